```python
import math
import jax, jax.numpy as jnp
from jax import lax
import numpy as np

D_MODEL = 4096
BATCH = 2
SEQ = 8192
DEPTH = 1

N_HEADS = 16
HEAD_DIM = 64
ATTN_WIDTH = N_HEADS * 2 * HEAD_DIM
SSM_WIDTH = D_MODEL // 2
SSM_GROUP = 16
N_GROUPS = SSM_WIDTH // SSM_GROUP
STATE = 64
D_FF = 11008
CONV_W = 3
Q_BLOCK = 128
SCAN_CHUNK = 128
EPS = 1e-6
IN_SPLITS = [ATTN_WIDTH, 2 * ATTN_WIDTH, 3 * ATTN_WIDTH, 3 * ATTN_WIDTH + SSM_WIDTH,
             3 * ATTN_WIDTH + SSM_WIDTH + D_MODEL]
IN_WIDTH = 3 * ATTN_WIDTH + SSM_WIDTH + 2 * D_MODEL

kernel_name = "hybrid_diffattn_s5_convffn_adaln"


def rmsnorm(x, g):
    xf = x.astype(jnp.float32)
    y = xf * lax.rsqrt(jnp.mean(xf * xf, axis=-1, keepdims=True) + EPS)
    return (y * g.astype(jnp.float32)).astype(x.dtype)


def diff_attention(q, k, v, lq1, lk1, lq2, lk2, subln_g, lam_init):
    B, L = q.shape[0], q.shape[1]
    nb = L // Q_BLOCK
    f32 = jnp.float32
    scale = HEAD_DIM ** -0.5
    lam = (jnp.exp(jnp.sum(lq1.astype(f32) * lk1.astype(f32)))
           - jnp.exp(jnp.sum(lq2.astype(f32) * lk2.astype(f32))) + lam_init)
    slopes = jnp.exp2(-8.0 * jnp.arange(1, N_HEADS + 1, dtype=f32) / N_HEADS)
    qb = q.astype(f32).reshape(B, nb, Q_BLOCK, N_HEADS, 2, HEAD_DIM).transpose(1, 0, 3, 4, 2, 5)
    kf = k.astype(f32).transpose(0, 2, 3, 1, 4)
    vf = v.astype(f32).transpose(0, 2, 1, 3)
    kpos = jnp.arange(L)

    def block(args):
        qblk, start = args
        s = jnp.einsum('bhjqd,bhjkd->bhjqk', qblk, kf) * scale
        qpos = start + jnp.arange(Q_BLOCK)
        dist = (qpos[:, None] - kpos[None, :]).astype(f32)
        bias = -slopes[:, None, None, None] * dist
        s = jnp.where(dist >= 0, s + bias, -jnp.inf)
        p = jax.nn.softmax(s, axis=-1)
        w = p[:, :, 0] - lam * p[:, :, 1]
        return jnp.einsum('bhqk,bhke->bhqe', w, vf)

    o = lax.map(block, (qb, jnp.arange(nb) * Q_BLOCK))
    o = rmsnorm(o, subln_g) * (1.0 - lam_init)
    return o.transpose(1, 0, 3, 2, 4).reshape(B, L, ATTN_WIDTH).astype(q.dtype)


def _ssm_combine(e1, e2):
    a1, b1 = e1
    a2, b2 = e2
    return a1 * a2, a2 * b1 + b2


def s5_branch(u, a_re, a_im, b_re, b_im, c_re, c_im, d_skip, log_dt, w_glu):
    B, L = u.shape[0], u.shape[1]
    f32 = jnp.float32
    nc = L // SCAN_CHUNK
    uf = u.astype(f32)
    u_chunks = uf.reshape(B, nc, SCAN_CHUNK, N_GROUPS, SSM_GROUP).transpose(1, 0, 2, 3, 4)
    lam = lax.complex(a_re.astype(f32), a_im.astype(f32))
    dt = jnp.exp(log_dt.astype(f32))[:, None]
    lam_dt = lam * dt
    a_bar = jnp.exp(lam_dt)
    b_mat = lax.complex(b_re.astype(f32), b_im.astype(f32))
    b_bar = ((a_bar - 1.0) / lam)[..., None] * b_mat
    c_mat = lax.complex(c_re.astype(f32), c_im.astype(f32))
    a_pow = jnp.exp(lam_dt[None] * jnp.arange(1, SCAN_CHUNK + 1, dtype=f32)[:, None, None])

    def chunk(carry, u_c):
        bu = jnp.einsum('gpc,btgc->btgp', b_bar, u_c.astype(jnp.complex64))
        a = jnp.broadcast_to(a_bar, bu.shape)
        _, xs = lax.associative_scan(_ssm_combine, (a, bu), axis=1)
        xs = xs + a_pow[None] * carry[:, None]
        y = jnp.einsum('gcp,btgp->btgc', c_mat, xs).real
        return xs[:, -1], y

    carry0 = jnp.zeros((B, N_GROUPS, STATE), jnp.complex64)
    _, y = lax.scan(chunk, carry0, u_chunks)
    y = y.transpose(1, 0, 2, 3, 4).reshape(B, L, SSM_WIDTH) + d_skip.astype(f32) * uf
    z = jax.nn.gelu(y)
    out = z * jax.nn.sigmoid(z @ w_glu.astype(f32))
    return out.astype(u.dtype)


def conv_ffn(h, w_up, conv_w, conv_b, w_down):
    L = h.shape[1]
    up = h @ w_up
    pad = jnp.pad(up, ((0, 0), (CONV_W - 1, 0), (0, 0)))
    conv = conv_b
    for j in range(CONV_W):
        conv = conv + conv_w[j] * pad[:, j:j + L]
    a, g = jnp.split(conv, 2, axis=-1)
    return (jax.nn.silu(g) * a) @ w_down


def setup_inputs(seed: int = 0) -> dict:
    key = jax.random.key(seed)
    ks = jax.random.split(key, 32)
    f32 = jnp.float32

    def nrm(k, shape, s):
        return jax.random.normal(k, shape, f32) * s

    P, G, c = STATE, N_GROUPS, SSM_GROUP
    return {
        "x": nrm(ks[0], (BATCH, SEQ, D_MODEL), 1.0),
        "c": nrm(ks[1], (BATCH, D_MODEL), 1.0),
        "ada_w": nrm(ks[2], (DEPTH, D_MODEL, 6 * D_MODEL), 0.5 * D_MODEL ** -0.5),
        "ada_b": nrm(ks[3], (DEPTH, 6 * D_MODEL), 0.01),
        "norm1_g": 1.0 + nrm(ks[4], (DEPTH, D_MODEL), 0.02),
        "w_in": nrm(ks[5], (DEPTH, D_MODEL, IN_WIDTH), D_MODEL ** -0.5),
        "lq1": nrm(ks[6], (DEPTH, HEAD_DIM), 0.1),
        "lk1": nrm(ks[7], (DEPTH, HEAD_DIM), 0.1),
        "lq2": nrm(ks[8], (DEPTH, HEAD_DIM), 0.1),
        "lk2": nrm(ks[9], (DEPTH, HEAD_DIM), 0.1),
        "subln_g": 1.0 + nrm(ks[10], (DEPTH, 2 * HEAD_DIM), 0.02),
        "a_re": -0.5 + nrm(ks[11], (DEPTH, G, P), 0.01),
        "a_im": math.pi * jnp.arange(P, dtype=f32) + nrm(ks[12], (DEPTH, G, P), 0.01),
        "b_re": nrm(ks[13], (DEPTH, G, P, c), (2 * c) ** -0.5),
        "b_im": nrm(ks[14], (DEPTH, G, P, c), (2 * c) ** -0.5),
        "c_re": nrm(ks[15], (DEPTH, G, c, P), (2 * P) ** -0.5 * 4.0),
        "c_im": nrm(ks[16], (DEPTH, G, c, P), (2 * P) ** -0.5 * 4.0),
        "d_skip": nrm(ks[17], (DEPTH, SSM_WIDTH), 1.0),
        "log_dt": jax.random.uniform(ks[18], (DEPTH, G), f32, math.log(0.001), math.log(0.1)),
        "w_glu": nrm(ks[19], (DEPTH, SSM_WIDTH, SSM_WIDTH), SSM_WIDTH ** -0.5),
        "w_attn_br": nrm(ks[20], (DEPTH, ATTN_WIDTH, D_MODEL), ATTN_WIDTH ** -0.5),
        "w_ssm_br": nrm(ks[21], (DEPTH, SSM_WIDTH, D_MODEL), SSM_WIDTH ** -0.5),
        "w_out": nrm(ks[22], (DEPTH, D_MODEL, D_MODEL), D_MODEL ** -0.5),
        "norm2_g": 1.0 + nrm(ks[23], (DEPTH, D_MODEL), 0.02),
        "w_up": nrm(ks[24], (DEPTH, D_MODEL, 2 * D_FF), D_MODEL ** -0.5),
        "conv_w": nrm(ks[25], (DEPTH, CONV_W, 2 * D_FF), CONV_W ** -0.5),
        "conv_b": nrm(ks[26], (DEPTH, 2 * D_FF), 0.01),
        "w_down": nrm(ks[27], (DEPTH, D_FF, D_MODEL), D_FF ** -0.5),
        "final_g": 1.0 + nrm(ks[28], (D_MODEL,), 0.02),
    }


def reference(x, c, ada_w, ada_b, norm1_g, w_in, lq1, lk1, lq2, lk2, subln_g,
              a_re, a_im, b_re, b_im, c_re, c_im, d_skip, log_dt, w_glu,
              w_attn_br, w_ssm_br, w_out, norm2_g, w_up, conv_w, conv_b, w_down,
              final_g):
    B, L = x.shape[0], x.shape[1]
    for l in range(DEPTH):
        lam_init = 0.8 - 0.6 * math.exp(-0.3 * l)
        mod = (c @ ada_w[l] + ada_b[l])[:, None, :]
        sh1, sc1, g1, sh2, sc2, g2 = jnp.split(mod, 6, axis=-1)

        h = rmsnorm(x, norm1_g[l]) * (1.0 + sc1) + sh1
        proj = h @ w_in[l]
        q, k, v, u, ga, gs = jnp.split(proj, IN_SPLITS, axis=-1)
        q = q.reshape(B, L, N_HEADS, 2, HEAD_DIM)
        k = k.reshape(B, L, N_HEADS, 2, HEAD_DIM)
        v = v.reshape(B, L, N_HEADS, 2 * HEAD_DIM)
        attn = diff_attention(q, k, v, lq1[l], lk1[l], lq2[l], lk2[l], subln_g[l], lam_init)
        ssm = s5_branch(u, a_re[l], a_im[l], b_re[l], b_im[l], c_re[l], c_im[l],
                        d_skip[l], log_dt[l], w_glu[l])
        merged = (jax.nn.sigmoid(ga) * (attn @ w_attn_br[l])
                  + jax.nn.sigmoid(gs) * (ssm @ w_ssm_br[l]))
        x = x + g1 * (merged @ w_out[l])

        h2 = rmsnorm(x, norm2_g[l]) * (1.0 + sc2) + sh2
        x = x + g2 * conv_ffn(h2, w_up[l], conv_w[l], conv_b[l], w_down[l])
    return rmsnorm(x, final_g)
```

```python
import functools
import math

import jax
import jax.numpy as jnp
from jax import lax
from jax.experimental import pallas as pl
from jax.experimental.pallas import tpu as pltpu

F32 = jnp.float32
BF16 = jnp.bfloat16
EPS = 1e-6
NEG_BIG = -1e30

V7X_VMEM_BYTES = 64 * 1024 * 1024
V7X_LANES = 128
V7X_MXU_DIM = 256
BF16_SUBLANES = 16
S5_CHUNK = 16
NORM_ROWS = 32


def _vmem_limit(block_bytes):
    return int(min(block_bytes + 12 * 1024 * 1024, V7X_VMEM_BYTES - 4 * 1024 * 1024))


def _tile(dim, want):
    t = min(dim, want)
    assert dim % t == 0, (dim, want)
    return t


def _params(vmem_bytes, n_grid):
    return pltpu.CompilerParams(
        dimension_semantics=("arbitrary",) * n_grid,
        vmem_limit_bytes=_vmem_limit(vmem_bytes),
    )


def _ada_kernel(c_ref, w_ref, b_ref, o_ref):
    w = w_ref[...].astype(BF16)
    o_ref[...] = jnp.dot(c_ref[...], w, preferred_element_type=F32) + b_ref[...]


def _ada_mod(c, ada_w, ada_b):
    B, D = c.shape
    n_out = ada_w.shape[1]
    rows = 8
    c_pad = jnp.zeros((rows, D), BF16).at[:B].set(c.astype(BF16))
    tn = _tile(n_out, 512)
    out = pl.pallas_call(
        _ada_kernel,
        grid=(n_out // tn,),
        in_specs=[
            pl.BlockSpec((rows, D), lambda j: (0, 0)),
            pl.BlockSpec((D, tn), lambda j: (0, j)),
            pl.BlockSpec((1, tn), lambda j: (0, j)),
        ],
        out_specs=pl.BlockSpec((rows, tn), lambda j: (0, j)),
        out_shape=jax.ShapeDtypeStruct((rows, n_out), F32),
        compiler_params=_params(2 * D * tn * 4 + D * tn * 2, 1),
        name="ada_mod",
    )(c_pad, ada_w, ada_b.reshape(1, n_out))
    return out[:B].reshape(B, 1, n_out)


def _adaln_rows(x_ref, g_ref, sc_ref, sh_ref, h_ref, n_rows, dst_row0):
    g = g_ref[...]
    one_sc = 1.0 + sc_ref[...]
    sh = sh_ref[...]
    rows = min(NORM_ROWS, n_rows)

    def body(r, carry):
        x = x_ref[pl.ds(r * rows, rows), :]
        ms = jnp.mean(x * x, axis=-1, keepdims=True)
        y = (x * lax.rsqrt(ms + EPS)) * g
        h_ref[pl.ds(dst_row0 + r * rows, rows), :] = (y * one_sc + sh).astype(BF16)
        return carry

    lax.fori_loop(0, n_rows // rows, body, 0)


def _in_proj_kernel(x_ref, g_ref, sc_ref, sh_ref, w_ref, o_ref, h_ref, *, tm):
    @pl.when(pl.program_id(1) == 0)
    def _():
        _adaln_rows(x_ref, g_ref, sc_ref, sh_ref, h_ref, tm, 0)

    o_ref[...] = jnp.dot(h_ref[...], w_ref[...], preferred_element_type=F32).astype(o_ref.dtype)


def _in_proj(x2d, norm_g, mod3, w_bf16, seq_len):
    N, D = x2d.shape
    n_out = w_bf16.shape[1]
    tm = _tile(seq_len, 1024)
    tn = _tile(n_out, 512)
    tpb = seq_len // tm
    vmem = 2 * tm * D * 4 + tm * D * 2 + 2 * D * tn * 2 + 2 * tm * tn * 2 + tm * tn * 4
    return pl.pallas_call(
        functools.partial(_in_proj_kernel, tm=tm),
        grid=(N // tm, n_out // tn),
        in_specs=[
            pl.BlockSpec((tm, D), lambda i, j: (i, 0)),
            pl.BlockSpec((1, D), lambda i, j: (0, 0)),
            pl.BlockSpec((None, 1, D), lambda i, j: (i // tpb, 0, 1)),
            pl.BlockSpec((None, 1, D), lambda i, j: (i // tpb, 0, 0)),
            pl.BlockSpec((D, tn), lambda i, j: (0, j)),
        ],
        out_specs=pl.BlockSpec((tm, tn), lambda i, j: (i, j)),
        out_shape=jax.ShapeDtypeStruct((N, n_out), BF16),
        scratch_shapes=[pltpu.VMEM((tm, D), BF16)],
        compiler_params=_params(vmem, 2),
        name="in_proj",
    )(x2d, norm_g.reshape(1, D), mod3, mod3, w_bf16)


def _attn_kernel(slope_ref, lam_ref, q_ref, k_ref, v_ref, g_ref, o_ref,
                 qs_ref, bias_ref, m_ref, l_ref, acc_ref, *, tq, d, out_scale):
    h = pl.program_id(1)
    qi = pl.program_id(2)
    slope = slope_ref[h]
    lam = lam_ref[0]
    tk = tq

    qf = q_ref[...].astype(F32) * (d ** -0.5)
    lane = lax.broadcasted_iota(jnp.int32, (tq, 2 * d), 1)
    qs_ref[0:tq, :] = jnp.where(lane < d, qf, 0.0).astype(BF16)
    qs_ref[tq:2 * tq, :] = jnp.where(lane >= d, qf, 0.0).astype(BF16)

    row = lax.broadcasted_iota(jnp.int32, (2 * tq, tk), 0)
    col = lax.broadcasted_iota(jnp.int32, (2 * tq, tk), 1)
    row = jnp.where(row >= tq, row - tq, row)
    delta = col - row
    bias_ref[...] = slope * delta.astype(F32)

    m_ref[...] = jnp.full(m_ref.shape, NEG_BIG, F32)
    l_ref[...] = jnp.zeros(l_ref.shape, F32)
    acc_ref[...] = jnp.zeros(acc_ref.shape, F32)

    def block(kb, masked):
        k = k_ref[pl.ds(kb * tk, tk), :]
        v = v_ref[pl.ds(kb * tk, tk), :]
        s = lax.dot_general(qs_ref[...], k, (((1,), (1,)), ((), ())),
                            preferred_element_type=F32)
        s = s + bias_ref[...]
        if masked:
            s = jnp.where(delta <= 0, s, NEG_BIG)
        off = slope * ((kb - qi) * tk).astype(F32)
        m_old = m_ref[...]
        m_new = jnp.maximum(m_old, jnp.max(s, axis=1, keepdims=True) + off)
        alpha = jnp.exp(m_old - m_new)
        p = jnp.exp(s - (m_new - off))
        l_ref[...] = alpha * l_ref[...] + jnp.sum(p, axis=1, keepdims=True)
        acc_ref[...] = alpha * acc_ref[...] + jnp.dot(p.astype(BF16), v, preferred_element_type=F32)
        m_ref[...] = m_new

    def full_block(kb, carry):
        block(kb, False)
        return carry

    lax.fori_loop(0, qi, full_block, 0)
    block(qi, True)

    o1 = acc_ref[0:tq, :] / l_ref[0:tq, :]
    o2 = acc_ref[tq:2 * tq, :] / l_ref[tq:2 * tq, :]
    o = o1 - lam * o2
    ms = jnp.mean(o * o, axis=-1, keepdims=True)
    y = (o * lax.rsqrt(ms + EPS)) * g_ref[...]
    o_ref[...] = (y * out_scale).astype(o_ref.dtype)


def _diff_attention(proj, lam, subln_g, batch, seq_len, n_heads, d, lam_init):
    N = proj.shape[0]
    hw = 2 * d
    assert hw == V7X_LANES
    tq = _tile(seq_len, 256)
    nq = seq_len // tq
    slopes = jnp.exp2(-8.0 * jnp.arange(1, n_heads + 1, dtype=F32) / n_heads)
    vmem = (2 * tq * hw * 2 + 2 * 2 * seq_len * hw * 2 + 2 * tq * hw * 2
            + 2 * tq * hw * 2 + 2 * tq * tq * 4 + 2 * 2 * tq * V7X_LANES * 4 + 2 * tq * hw * 4
            + 4 * 2 * tq * tq * 4)
    grid_spec = pltpu.PrefetchScalarGridSpec(
        num_scalar_prefetch=2,
        grid=(batch, n_heads, nq),
        in_specs=[
            pl.BlockSpec((tq, hw), lambda b, h, qi, *_: (b * nq + qi, h)),
            pl.BlockSpec((seq_len, hw), lambda b, h, qi, *_: (b, n_heads + h)),
            pl.BlockSpec((seq_len, hw), lambda b, h, qi, *_: (b, 2 * n_heads + h)),
            pl.BlockSpec((1, hw), lambda b, h, qi, *_: (0, 0)),
        ],
        out_specs=pl.BlockSpec((tq, hw), lambda b, h, qi, *_: (b * nq + qi, h)),
        scratch_shapes=[
            pltpu.VMEM((2 * tq, hw), BF16),
            pltpu.VMEM((2 * tq, tq), F32),
            pltpu.VMEM((2 * tq, 1), F32),
            pltpu.VMEM((2 * tq, 1), F32),
            pltpu.VMEM((2 * tq, hw), F32),
        ],
    )
    return pl.pallas_call(
        functools.partial(_attn_kernel, tq=tq, d=d, out_scale=1.0 - lam_init),
        grid_spec=grid_spec,
        out_shape=jax.ShapeDtypeStruct((N, n_heads * hw), BF16),
        compiler_params=_params(vmem, 3),
        name="diff_attn",
    )(slopes, lam.reshape(1), proj, proj, proj, subln_g.reshape(1, hw))


def _s5_tables(a_re, a_im, b_re, b_im, c_re, c_im, d_skip, log_dt):
    hp = lax.Precision.HIGHEST
    G, P = a_re.shape
    cch = b_re.shape[-1]
    T = S5_CHUNK
    lam = lax.complex(a_re.astype(F32), a_im.astype(F32))
    dt = jnp.exp(log_dt.astype(F32))[:, None]
    lam_dt = lam * dt
    a_bar = jnp.exp(lam_dt)
    b_bar = ((a_bar - 1.0) / lam)[..., None] * lax.complex(b_re.astype(F32), b_im.astype(F32))
    c_mat = lax.complex(c_re.astype(F32), c_im.astype(F32))
    pw = jnp.exp(lam_dt[None] * jnp.arange(0, T + 1, dtype=F32)[:, None, None])

    w_c = jnp.einsum('sgp,gpc->gscp', pw[T - 1 - jnp.arange(T)], b_bar).reshape(G, T * cch, P)
    w_tab = jnp.concatenate([w_c.real, w_c.imag, w_c.imag, w_c.real], axis=-1)

    m_c = jnp.einsum('gcp,tgp->gptc', c_mat, pw[1:T + 1]).reshape(G, P, T * cch)
    cp_tab = jnp.concatenate([m_c.real, -m_c.imag], axis=1)

    cr, ci = c_mat.real, c_mat.imag
    ab = pw[:T][:, :, :, None] * b_bar[None]
    k_tab = (jnp.einsum('gcp,kgpd->kgcd', cr, ab.real, precision=hp)
             - jnp.einsum('gcp,kgpd->kgcd', ci, ab.imag, precision=hp))
    k_tab = k_tab.at[0].add(jnp.eye(cch, dtype=F32)[None] * d_skip.astype(F32).reshape(G, cch, 1))
    lag = jnp.arange(T)[None, :] - jnp.arange(T)[:, None]
    tz = k_tab[jnp.clip(lag, 0, T - 1)]
    tz = jnp.where((lag >= 0)[:, :, None, None, None], tz, 0.0)
    tz_tab = tz.transpose(2, 0, 4, 1, 3).reshape(G, T * cch, T * cch)

    at = pw[T]
    a_tab = jnp.stack([
        jnp.concatenate([at.real, at.real], axis=-1),
        jnp.concatenate([-at.imag, at.imag], axis=-1),
        jnp.concatenate([at.imag, -at.imag], axis=-1),
    ], axis=1)
    return w_tab.astype(BF16), cp_tab.astype(BF16), tz_tab.astype(BF16), a_tab.astype(F32)


def _s5_kernel(u_ref, w_ref, cp_ref, tz_ref, a_ref, y_ref, c_ref, s_ref, *, gb, n_batch, nch):
    p2 = s_ref.shape[-1]
    for g in range(gb):
        c_ref[g] = jnp.dot(u_ref[g], w_ref[g], preferred_element_type=F32)

    a1 = [a_ref[g, 0:1, :] for g in range(gb)]
    a2 = [a_ref[g, 1:2, :] for g in range(gb)]
    a3 = [a_ref[g, 2:3, :] for g in range(gb)]
    chains = [(g, b) for g in range(gb) for b in range(n_batch)]

    def step(j, carry):
        out = []
        for (g, b), (x, xs) in zip(chains, carry):
            r = b * nch + j
            s_ref[g, pl.ds(r, 1), :] = x
            cc = c_ref[g, pl.ds(r, 1), :]
            out.append((a1[g] * x + a2[g] * xs + cc[:, :p2],
                        a1[g] * xs + a3[g] * x + cc[:, p2:]))
        return tuple(out)

    zero = jnp.zeros((1, p2), F32)
    lax.fori_loop(0, nch, step, tuple((zero, zero) for _ in chains))

    for g in range(gb):
        y = jnp.dot(s_ref[g].astype(BF16), cp_ref[g], preferred_element_type=F32)
        y = y + jnp.dot(u_ref[g], tz_ref[g], preferred_element_type=F32)
        y_ref[g] = y.astype(y_ref.dtype)


def _s5_scan(u, tables, batch, seq_len):
    w_tab, cp_tab, tz_tab, a_tab = tables
    N = u.shape[0]
    G = w_tab.shape[0]
    tc = w_tab.shape[1]
    p2 = cp_tab.shape[1]
    cch = tc // S5_CHUNK
    nch = seq_len // S5_CHUNK
    rows = batch * nch
    u_g = u.reshape(rows, S5_CHUNK, G, cch).transpose(2, 0, 1, 3).reshape(G, rows, tc)
    gb = _tile(G, 8)
    vmem = (2 * 2 * gb * rows * tc * 2 + 2 * gb * (tc * 2 * p2 + p2 * tc + tc * tc) * 2
            + gb * rows * 2 * p2 * 4 + gb * rows * p2 * 4 + 2 * rows * tc * 4)
    y_g = pl.pallas_call(
        functools.partial(_s5_kernel, gb=gb, n_batch=batch, nch=nch),
        grid=(G // gb,),
        in_specs=[
            pl.BlockSpec((gb, rows, tc), lambda i: (i, 0, 0)),
            pl.BlockSpec((gb, tc, 2 * p2), lambda i: (i, 0, 0)),
            pl.BlockSpec((gb, p2, tc), lambda i: (i, 0, 0)),
            pl.BlockSpec((gb, tc, tc), lambda i: (i, 0, 0)),
            pl.BlockSpec((gb, 3, p2), lambda i: (i, 0, 0)),
        ],
        out_specs=pl.BlockSpec((gb, rows, tc), lambda i: (i, 0, 0)),
        out_shape=jax.ShapeDtypeStruct((G, rows, tc), BF16),
        scratch_shapes=[
            pltpu.VMEM((gb, rows, 2 * p2), F32),
            pltpu.VMEM((gb, rows, p2), F32),
        ],
        compiler_params=_params(vmem, 1),
        name="s5_scan",
    )(u_g, w_tab, cp_tab, tz_tab, a_tab)
    return y_g.reshape(G, rows, S5_CHUNK, cch).transpose(1, 2, 0, 3).reshape(N, G * cch)


def _gelu_tanh(x):
    return 0.5 * x * (1.0 + jnp.tanh(math.sqrt(2.0 / math.pi) * (x + 0.044715 * (x * x * x))))


def _glu_kernel(y_ref, w_ref, o_ref):
    z = _gelu_tanh(y_ref[...].astype(F32))
    gate = jnp.dot(z.astype(BF16), w_ref[...], preferred_element_type=F32)
    o_ref[...] = (z * jax.nn.sigmoid(gate)).astype(o_ref.dtype)


def _glu(y, w_bf16):
    N, W = y.shape
    tm = _tile(N, 512)
    vmem = 2 * tm * W * 2 + 2 * W * W * 2 + 2 * tm * W * 2 + 4 * tm * W * 4
    return pl.pallas_call(
        _glu_kernel,
        grid=(N // tm,),
        in_specs=[pl.BlockSpec((tm, W), lambda i: (i, 0)),
                  pl.BlockSpec((W, W), lambda i: (0, 0))],
        out_specs=pl.BlockSpec((tm, W), lambda i: (i, 0)),
        out_shape=jax.ShapeDtypeStruct((N, W), BF16),
        compiler_params=_params(vmem, 1),
        name="glu",
    )(y, w_bf16)


def _merge_kernel(a_ref, s_ref, wa_ref, ws_ref, ga_ref, gs_ref, o_ref):
    ya = jnp.dot(a_ref[...], wa_ref[...], preferred_element_type=F32)
    ys = jnp.dot(s_ref[...], ws_ref[...], preferred_element_type=F32)
    o = jax.nn.sigmoid(ga_ref[...].astype(F32)) * ya + jax.nn.sigmoid(gs_ref[...].astype(F32)) * ys
    o_ref[...] = o.astype(o_ref.dtype)


def _merge(attn, ssm, wa_bf16, ws_bf16, proj, ga_col0, gs_col0):
    N, AW = attn.shape
    SW = ssm.shape[1]
    D = wa_bf16.shape[1]
    tm = _tile(N, 1024)
    tn = _tile(D, 512)
    assert ga_col0 % tn == 0 and gs_col0 % tn == 0
    ga_blk, gs_blk = ga_col0 // tn, gs_col0 // tn
    vmem = (2 * tm * (AW + SW) * 2 + 2 * (AW + SW) * tn * 2 + 3 * 2 * tm * tn * 2 + 3 * tm * tn * 4)
    return pl.pallas_call(
        _merge_kernel,
        grid=(N // tm, D // tn),
        in_specs=[
            pl.BlockSpec((tm, AW), lambda i, j: (i, 0)),
            pl.BlockSpec((tm, SW), lambda i, j: (i, 0)),
            pl.BlockSpec((AW, tn), lambda i, j: (0, j)),
            pl.BlockSpec((SW, tn), lambda i, j: (0, j)),
            pl.BlockSpec((tm, tn), lambda i, j: (i, ga_blk + j)),
            pl.BlockSpec((tm, tn), lambda i, j: (i, gs_blk + j)),
        ],
        out_specs=pl.BlockSpec((tm, tn), lambda i, j: (i, j)),
        out_shape=jax.ShapeDtypeStruct((N, D), BF16),
        compiler_params=_params(vmem, 2),
        name="merge",
    )(attn, ssm, wa_bf16, ws_bf16, proj, proj)


def _out_proj_kernel(m_ref, w_ref, x_ref, g_ref, o_ref):
    o_ref[...] = x_ref[...] + g_ref[...] * jnp.dot(m_ref[...], w_ref[...], preferred_element_type=F32)


def _out_proj(merged, w_bf16, x2d, mod3, seq_len):
    N, D = x2d.shape
    K = merged.shape[1]
    tm = _tile(seq_len, 1024)
    tn = _tile(D, 512)
    tpb = seq_len // tm
    g1_blk = 2 * (D // tn)
    vmem = 2 * tm * K * 2 + 2 * K * tn * 2 + 4 * tm * tn * 4 + tm * tn * 4
    return pl.pallas_call(
        _out_proj_kernel,
        grid=(N // tm, D // tn),
        in_specs=[
            pl.BlockSpec((tm, K), lambda i, j: (i, 0)),
            pl.BlockSpec((K, tn), lambda i, j: (0, j)),
            pl.BlockSpec((tm, tn), lambda i, j: (i, j)),
            pl.BlockSpec((None, 1, tn), lambda i, j: (i // tpb, 0, g1_blk + j)),
        ],
        out_specs=pl.BlockSpec((tm, tn), lambda i, j: (i, j)),
        out_shape=jax.ShapeDtypeStruct((N, D), F32),
        compiler_params=_params(vmem, 2),
        name="out_proj",
    )(merged, w_bf16, x2d, mod3)


def _ffn_up_kernel(x_ref, halo_ref, g_ref, sc_ref, sh_ref, wa_ref, wg_ref, cwa_ref, cwg_ref,
                   o_ref, h_ref, ua_ref, ug_ref, *, tm, halo, seq_len, taps):
    i = pl.program_id(0)

    @pl.when(pl.program_id(1) == 0)
    def _():
        _adaln_rows(x_ref, g_ref, sc_ref, sh_ref, h_ref, tm, halo)
        _adaln_rows(halo_ref, g_ref, sc_ref, sh_ref, h_ref, halo, 0)

        @pl.when((i * tm) % seq_len == 0)
        def _():
            h_ref[0:halo, :] = jnp.zeros((halo, h_ref.shape[1]), BF16)

    h = h_ref[...]
    ua_ref[...] = jnp.dot(h, wa_ref[...], preferred_element_type=F32)
    ug_ref[...] = jnp.dot(h, wg_ref[...], preferred_element_type=F32)

    def conv(u_ref, cw_ref):
        out = cw_ref[taps:taps + 1, :]
        for j in range(taps):
            out = out + cw_ref[j:j + 1, :] * u_ref[pl.ds(halo - (taps - 1) + j, tm), :]
        return out

    a = conv(ua_ref, cwa_ref)
    g = conv(ug_ref, cwg_ref)
    o_ref[...] = (g * jax.nn.sigmoid(g) * a).astype(o_ref.dtype)


def _ffn_up(x1, norm_g, mod3, w_up_bf16, conv_tab, seq_len, ffp, taps):
    N, D = x1.shape
    tm = _tile(seq_len, 1024)
    tn = _tile(ffp, 512)
    halo = BF16_SUBLANES
    assert taps - 1 <= halo and tm % halo == 0
    tpb = seq_len // tm
    nj = ffp // tn
    hb = tm // halo
    vmem = (tm * D * 4 + 2 * halo * D * 4 + (tm + halo) * D * 2 + 2 * 2 * D * tn * 2
            + 2 * tm * tn * 2 + 2 * (tm + halo) * tn * 4 + 4 * tm * tn * 4)
    return pl.pallas_call(
        functools.partial(_ffn_up_kernel, tm=tm, halo=halo, seq_len=seq_len, taps=taps),
        grid=(N // tm, nj),
        in_specs=[
            pl.BlockSpec((tm, D), lambda i, j: (i, 0), pipeline_mode=pl.Buffered(1)),
            pl.BlockSpec((halo, D), lambda i, j: (jnp.maximum(i * hb - 1, 0), 0)),
            pl.BlockSpec((1, D), lambda i, j: (0, 0)),
            pl.BlockSpec((None, 1, D), lambda i, j: (i // tpb, 0, 4)),
            pl.BlockSpec((None, 1, D), lambda i, j: (i // tpb, 0, 3)),
            pl.BlockSpec((D, tn), lambda i, j: (0, j)),
            pl.BlockSpec((D, tn), lambda i, j: (0, nj + j)),
            pl.BlockSpec((8, tn), lambda i, j: (0, j)),
            pl.BlockSpec((8, tn), lambda i, j: (0, nj + j)),
        ],
        out_specs=pl.BlockSpec((tm, tn), lambda i, j: (i, j)),
        out_shape=jax.ShapeDtypeStruct((N, ffp), BF16),
        scratch_shapes=[
            pltpu.VMEM((tm + halo, D), BF16),
            pltpu.VMEM((tm + halo, tn), F32),
            pltpu.VMEM((tm + halo, tn), F32),
        ],
        compiler_params=_params(vmem, 2),
        name="ffn_up",
    )(x1, x1, norm_g.reshape(1, D), mod3, mod3, w_up_bf16, w_up_bf16, conv_tab, conv_tab)


def _ffn_down_kernel(a_ref, w_ref, x_ref, g_ref, fg_ref, o_ref, *, tm, nk, final_norm):
    k = pl.program_id(1)

    @pl.when(k == 0)
    def _():
        o_ref[...] = jnp.zeros(o_ref.shape, F32)

    o_ref[...] += jnp.dot(a_ref[...], w_ref[...], preferred_element_type=F32)

    @pl.when(k == nk - 1)
    def _():
        gate = g_ref[...]
        fg = fg_ref[...]
        rows = min(NORM_ROWS, tm)

        def body(r, carry):
            sl = pl.ds(r * rows, rows)
            x2 = x_ref[sl, :] + gate * o_ref[sl, :]
            if final_norm:
                ms = jnp.mean(x2 * x2, axis=-1, keepdims=True)
                x2 = (x2 * lax.rsqrt(ms + EPS)) * fg
            o_ref[sl, :] = x2
            return carry

        lax.fori_loop(0, tm // rows, body, 0)


def _ffn_down(act, w_bf16, x1, mod3, final_g, seq_len, final_norm):
    N, D = x1.shape
    K = act.shape[1]
    tm = _tile(seq_len, 512)
    tk = _tile(K, 1024)
    tpb = seq_len // tm
    nk = K // tk
    vmem = 2 * tm * tk * 2 + 2 * tk * D * 2 + tm * D * 4 + 2 * tm * D * 4 + tm * D * 4
    return pl.pallas_call(
        functools.partial(_ffn_down_kernel, tm=tm, nk=nk, final_norm=final_norm),
        grid=(N // tm, nk),
        in_specs=[
            pl.BlockSpec((tm, tk), lambda i, k: (i, k)),
            pl.BlockSpec((tk, D), lambda i, k: (k, 0)),
            pl.BlockSpec((tm, D), lambda i, k: (i, 0), pipeline_mode=pl.Buffered(1)),
            pl.BlockSpec((None, 1, D), lambda i, k: (i // tpb, 0, 5)),
            pl.BlockSpec((1, D), lambda i, k: (0, 0)),
        ],
        out_specs=pl.BlockSpec((tm, D), lambda i, k: (i, 0)),
        out_shape=jax.ShapeDtypeStruct((N, D), F32),
        compiler_params=_params(vmem, 2),
        name="ffn_down",
    )(act, w_bf16, x1, mod3, final_g.reshape(1, D))


def _round_up(n, m):
    return (n + m - 1) // m * m


def kernel(x, c, ada_w, ada_b, norm1_g, w_in, lq1, lk1, lq2, lk2, subln_g, a_re, a_im, b_re, b_im,
           c_re, c_im, d_skip, log_dt, w_glu, w_attn_br, w_ssm_br, w_out, norm2_g, w_up, conv_w,
           conv_b, w_down, final_g):
    B, L, D = x.shape
    depth = ada_w.shape[0]
    d = lq1.shape[-1]
    aw = w_attn_br.shape[1]
    sw = w_ssm_br.shape[1]
    n_heads = aw // (2 * d)
    ff = w_down.shape[1]
    taps = conv_w.shape[1]
    ffp = _round_up(ff, 1024) if ff > 1024 else ff
    u_col0 = 3 * aw
    ga_col0 = u_col0 + sw
    gs_col0 = ga_col0 + D

    xc = x.reshape(B * L, D)
    for l in range(depth):
        lam_init = 0.8 - 0.6 * math.exp(-0.3 * l)
        lam = (jnp.exp(jnp.sum(lq1[l].astype(F32) * lk1[l].astype(F32)))
               - jnp.exp(jnp.sum(lq2[l].astype(F32) * lk2[l].astype(F32))) + lam_init)
        mod3 = _ada_mod(c, ada_w[l], ada_b[l])

        proj = _in_proj(xc, norm1_g[l], mod3, w_in[l].astype(BF16), L)
        attn = _diff_attention(proj, lam, subln_g[l], B, L, n_heads, d, lam_init)
        tables = _s5_tables(a_re[l], a_im[l], b_re[l], b_im[l], c_re[l], c_im[l], d_skip[l], log_dt[l])
        y = _s5_scan(proj[:, u_col0:u_col0 + sw], tables, B, L)
        ssm = _glu(y, w_glu[l].astype(BF16))
        merged = _merge(attn, ssm, w_attn_br[l].astype(BF16), w_ssm_br[l].astype(BF16),
                        proj, ga_col0, gs_col0)
        x1 = _out_proj(merged, w_out[l].astype(BF16), xc, mod3, L)

        pad = ffp - ff
        w_up_p = jnp.concatenate([jnp.pad(w_up[l][:, :ff], ((0, 0), (0, pad))),
                                  jnp.pad(w_up[l][:, ff:], ((0, 0), (0, pad)))], axis=1).astype(BF16)
        conv_rows = jnp.concatenate([conv_w[l], conv_b[l][None]], axis=0).astype(F32)
        conv_tab = jnp.concatenate([jnp.pad(conv_rows[:, :ff], ((0, 8 - taps - 1), (0, pad))),
                                    jnp.pad(conv_rows[:, ff:], ((0, 8 - taps - 1), (0, pad)))], axis=1)
        w_down_p = jnp.pad(w_down[l], ((0, pad), (0, 0))).astype(BF16)
        act = _ffn_up(x1, norm2_g[l], mod3, w_up_p, conv_tab, L, ffp, taps)
        xc = _ffn_down(act, w_down_p, x1, mod3, final_g, L, final_norm=(l == depth - 1))
    return xc.reshape(B, L, D)
```

```python
import functools
import math

import jax
import jax.numpy as jnp
from jax import lax
from jax.experimental import pallas as pl
from jax.experimental.pallas import tpu as pltpu

F32 = jnp.float32
BF16 = jnp.bfloat16
EPS = 1e-6
NEG_BIG = -1e30
LOG2E = 1.0 / math.log(2.0)

V7X_VMEM_BYTES = 64 * 1024 * 1024
V7X_LANES = 128
V7X_MXU_DIM = 256
BF16_SUBLANES = 16
S5_CHUNK = 16
NORM_ROWS = 32
ATTN_BLOCK = 512


def _vmem_limit(block_bytes):
    return int(min(block_bytes + 12 * 1024 * 1024, V7X_VMEM_BYTES - 4 * 1024 * 1024))


def _tile(dim, want):
    t = min(dim, want)
    assert dim % t == 0, (dim, want)
    return t


def _params(vmem_bytes, n_grid):
    return pltpu.CompilerParams(
        dimension_semantics=("arbitrary",) * n_grid,
        vmem_limit_bytes=_vmem_limit(vmem_bytes),
    )


def _ada_kernel(c_ref, w_ref, b_ref, o_ref):
    w = w_ref[...].astype(BF16)
    o_ref[...] = jnp.dot(c_ref[...], w, preferred_element_type=F32) + b_ref[...]


def _ada_mod(c, ada_w, ada_b):
    B, D = c.shape
    n_out = ada_w.shape[1]
    rows = 8
    c_pad = jnp.zeros((rows, D), BF16).at[:B].set(c.astype(BF16))
    tn = _tile(n_out, 512)
    out = pl.pallas_call(
        _ada_kernel,
        grid=(n_out // tn,),
        in_specs=[
            pl.BlockSpec((rows, D), lambda j: (0, 0)),
            pl.BlockSpec((D, tn), lambda j: (0, j)),
            pl.BlockSpec((1, tn), lambda j: (0, j)),
        ],
        out_specs=pl.BlockSpec((rows, tn), lambda j: (0, j)),
        out_shape=jax.ShapeDtypeStruct((rows, n_out), F32),
        compiler_params=_params(2 * D * tn * 4 + D * tn * 2, 1),
        name="ada_mod",
    )(c_pad, ada_w, ada_b.reshape(1, n_out))
    return out[:B].reshape(B, 1, n_out)


def _adaln_rows(x_ref, g_ref, sc_ref, sh_ref, h_ref, n_rows, dst_row0):
    g = g_ref[...]
    one_sc = 1.0 + sc_ref[...]
    sh = sh_ref[...]
    rows = min(NORM_ROWS, n_rows)

    def body(r, carry):
        x = x_ref[pl.ds(r * rows, rows), :]
        ms = jnp.mean(x * x, axis=-1, keepdims=True)
        y = (x * lax.rsqrt(ms + EPS)) * g
        h_ref[pl.ds(dst_row0 + r * rows, rows), :] = (y * one_sc + sh).astype(BF16)
        return carry

    lax.fori_loop(0, n_rows // rows, body, 0)


def _in_proj_kernel(x_ref, g_ref, sc_ref, sh_ref, w_ref, o_ref, h_ref, *, tm):
    @pl.when(pl.program_id(1) == 0)
    def _():
        _adaln_rows(x_ref, g_ref, sc_ref, sh_ref, h_ref, tm, 0)

    o_ref[...] = jnp.dot(h_ref[...], w_ref[...], preferred_element_type=F32).astype(o_ref.dtype)


def _in_proj(x2d, norm_g, mod3, w_bf16, seq_len):
    N, D = x2d.shape
    n_out = w_bf16.shape[1]
    tm = _tile(seq_len, 1024)
    tn = _tile(n_out, 512)
    tpb = seq_len // tm
    vmem = 2 * tm * D * 4 + tm * D * 2 + 2 * D * tn * 2 + 2 * tm * tn * 2 + tm * tn * 4
    return pl.pallas_call(
        functools.partial(_in_proj_kernel, tm=tm),
        grid=(N // tm, n_out // tn),
        in_specs=[
            pl.BlockSpec((tm, D), lambda i, j: (i, 0)),
            pl.BlockSpec((1, D), lambda i, j: (0, 0)),
            pl.BlockSpec((None, 1, D), lambda i, j: (i // tpb, 0, 1)),
            pl.BlockSpec((None, 1, D), lambda i, j: (i // tpb, 0, 0)),
            pl.BlockSpec((D, tn), lambda i, j: (0, j)),
        ],
        out_specs=pl.BlockSpec((tm, tn), lambda i, j: (i, j)),
        out_shape=jax.ShapeDtypeStruct((N, n_out), BF16),
        scratch_shapes=[pltpu.VMEM((tm, D), BF16)],
        compiler_params=_params(vmem, 2),
        name="in_proj",
    )(x2d, norm_g.reshape(1, D), mod3, mod3, w_bf16)


def _attn_kernel(slope_ref, lam_ref, q_ref, k_ref, v_ref, g_ref, o_ref,
                 qa_ref, ek_ref, vt_ref, s_ref, m_ref, l_ref, acc_ref, *, tq, d, out_scale):
    h = pl.program_id(1)
    qi = pl.program_id(2)
    lam = lam_ref[0]
    tk = tq
    hw = 2 * d
    nkb = vt_ref.shape[0]
    slope2 = slope_ref[h] * LOG2E

    @pl.when(qi == 0)
    def _():
        for c in range(nkb):
            vt_ref[c] = v_ref[c * tk:(c + 1) * tk, :].astype(F32).T.astype(BF16)
        jj = lax.broadcasted_iota(jnp.int32, (tk, hw), 0)
        ln = lax.broadcasted_iota(jnp.int32, (tk, hw), 1)
        hi = ((jj // 32) * 32).astype(F32)
        lo = (jj % 32).astype(F32)
        ek_ref[...] = jnp.where(ln < 3, hi, jnp.where(ln < 6, lo, 0.0)).astype(BF16)

    qf = q_ref[...].astype(F32) * (d ** -0.5 * LOG2E)
    lane = lax.broadcasted_iota(jnp.int32, (tq, hw), 1)
    qa_ref[0:tq, 0:hw] = jnp.where(lane < d, qf, 0.0).astype(BF16)
    qa_ref[tq:2 * tq, 0:hw] = jnp.where(lane >= d, qf, 0.0).astype(BF16)
    sl = jnp.full((1, hw), slope2, F32)
    pa = sl.astype(BF16).astype(F32)
    ra = sl - pa
    pb = ra.astype(BF16).astype(F32)
    pc = (ra - pb).astype(BF16).astype(F32)
    l1 = lax.broadcasted_iota(jnp.int32, (1, hw), 1)
    piece = jnp.where(l1 % 3 == 0, pa, jnp.where(l1 % 3 == 1, pb, pc))
    eq = jnp.where(l1 < 6, piece, 0.0)
    qa_ref[:, hw:2 * hw] = jnp.broadcast_to(eq, (2 * tq, hw)).astype(BF16)

    m_ref[...] = jnp.full(m_ref.shape, NEG_BIG, F32)
    l_ref[...] = jnp.zeros(l_ref.shape, F32)
    acc_ref[...] = jnp.zeros(acc_ref.shape, F32)

    def scores(kb, slot):
        k = k_ref[pl.ds(pl.multiple_of(kb * tk, tk), tk), :]
        ka = jnp.concatenate([k, ek_ref[...]], axis=1)
        s_ref[slot] = lax.dot_general(ka, qa_ref[...], (((1,), (1,)), ((), ())),
                                      preferred_element_type=F32)

    def softmax_pv(kb, slot, masked):
        s = s_ref[slot]
        if masked:
            key = lax.broadcasted_iota(jnp.int32, (tk, 2 * tq), 0)
            qry = lax.broadcasted_iota(jnp.int32, (tk, 2 * tq), 1)
            qry = jnp.where(qry >= tq, qry - tq, qry)
            s = jnp.where(key <= qry, s, NEG_BIG)
        off = slope2 * ((kb - qi) * tk).astype(F32)
        m_old = m_ref[...]
        m_new = jnp.maximum(m_old, jnp.max(s, axis=0, keepdims=True) + off)
        alpha = jnp.exp2(m_old - m_new)
        p = jnp.exp2(s - (m_new - off))
        l_ref[...] = alpha * l_ref[...] + jnp.sum(p, axis=0, keepdims=True)
        pv = jnp.dot(vt_ref[kb], p.astype(BF16), preferred_element_type=F32)
        acc_ref[...] = alpha * acc_ref[...] + pv
        m_ref[...] = m_new

    scores(0, 0)

    def pair(i, carry):
        kb = 2 * i
        scores(kb + 1, 1)
        softmax_pv(kb, 0, False)
        scores(kb + 2, 0)
        softmax_pv(kb + 1, 1, False)
        return carry

    lax.fori_loop(0, qi // 2, pair, 0)

    @pl.when(qi % 2 == 0)
    def _():
        softmax_pv(qi, 0, True)

    @pl.when(qi % 2 == 1)
    def _():
        scores(qi, 1)
        softmax_pv(qi - 1, 0, False)
        softmax_pv(qi, 1, True)

    inv_l = 1.0 / l_ref[...]
    o = acc_ref[:, 0:tq] * inv_l[:, 0:tq] - lam * (acc_ref[:, tq:2 * tq] * inv_l[:, tq:2 * tq])
    ms = jnp.mean(o * o, axis=0, keepdims=True)
    y = (o * lax.rsqrt(ms + EPS)).T
    o_ref[...] = (y * g_ref[...] * out_scale).astype(o_ref.dtype)


def _diff_attention(proj, lam, subln_g, batch, seq_len, n_heads, d, lam_init):
    N = proj.shape[0]
    hw = 2 * d
    assert hw == V7X_LANES
    tq = _tile(seq_len, ATTN_BLOCK)
    assert tq % 32 == 0 and tq <= 32 * 256
    nq = seq_len // tq
    slopes = jnp.exp2(-8.0 * jnp.arange(1, n_heads + 1, dtype=F32) / n_heads)
    vmem = (2 * 2 * tq * hw * 2 + 2 * 2 * seq_len * hw * 2 + seq_len * hw * 2
            + 2 * tq * 2 * hw * 2 + tq * hw * 2 + 2 * tq * 2 * tq * 4
            + hw * 2 * tq * 4 + 3 * tq * 2 * tq * 4)
    grid_spec = pltpu.PrefetchScalarGridSpec(
        num_scalar_prefetch=2,
        grid=(batch, n_heads, nq),
        in_specs=[
            pl.BlockSpec((tq, hw), lambda b, h, qi, *_: (b * nq + qi, h)),
            pl.BlockSpec((seq_len, hw), lambda b, h, qi, *_: (b, n_heads + h)),
            pl.BlockSpec((seq_len, hw), lambda b, h, qi, *_: (b, 2 * n_heads + h)),
            pl.BlockSpec((1, hw), lambda b, h, qi, *_: (0, 0)),
        ],
        out_specs=pl.BlockSpec((tq, hw), lambda b, h, qi, *_: (b * nq + qi, h)),
        scratch_shapes=[
            pltpu.VMEM((2 * tq, 2 * hw), BF16),
            pltpu.VMEM((tq, hw), BF16),
            pltpu.VMEM((seq_len // tq, hw, tq), BF16),
            pltpu.VMEM((2, tq, 2 * tq), F32),
            pltpu.VMEM((1, 2 * tq), F32),
            pltpu.VMEM((1, 2 * tq), F32),
            pltpu.VMEM((hw, 2 * tq), F32),
        ],
    )
    return pl.pallas_call(
        functools.partial(_attn_kernel, tq=tq, d=d, out_scale=1.0 - lam_init),
        grid_spec=grid_spec,
        out_shape=jax.ShapeDtypeStruct((N, n_heads * hw), BF16),
        compiler_params=_params(vmem, 3),
        name="diff_attn",
    )(slopes, lam.reshape(1), proj, proj, proj, subln_g.reshape(1, hw))


def _s5_tables(a_re, a_im, b_re, b_im, c_re, c_im, d_skip, log_dt):
    hp = lax.Precision.HIGHEST
    G, P = a_re.shape
    cch = b_re.shape[-1]
    T = S5_CHUNK
    lam = lax.complex(a_re.astype(F32), a_im.astype(F32))
    dt = jnp.exp(log_dt.astype(F32))[:, None]
    lam_dt = lam * dt
    a_bar = jnp.exp(lam_dt)
    b_bar = ((a_bar - 1.0) / lam)[..., None] * lax.complex(b_re.astype(F32), b_im.astype(F32))
    c_mat = lax.complex(c_re.astype(F32), c_im.astype(F32))
    pw = jnp.exp(lam_dt[None] * jnp.arange(0, T + 1, dtype=F32)[:, None, None])

    w_c = jnp.einsum('sgp,gpc->gscp', pw[T - 1 - jnp.arange(T)], b_bar).reshape(G, T * cch, P)
    w_tab = jnp.concatenate([w_c.real, w_c.imag, w_c.imag, w_c.real], axis=-1)

    m_c = jnp.einsum('gcp,tgp->gptc', c_mat, pw[1:T + 1]).reshape(G, P, T * cch)
    cp_tab = jnp.concatenate([m_c.real, -m_c.imag], axis=1)

    cr, ci = c_mat.real, c_mat.imag
    ab = pw[:T][:, :, :, None] * b_bar[None]
    k_tab = (jnp.einsum('gcp,kgpd->kgcd', cr, ab.real, precision=hp)
             - jnp.einsum('gcp,kgpd->kgcd', ci, ab.imag, precision=hp))
    k_tab = k_tab.at[0].add(jnp.eye(cch, dtype=F32)[None] * d_skip.astype(F32).reshape(G, cch, 1))
    lag = jnp.arange(T)[None, :] - jnp.arange(T)[:, None]
    tz = k_tab[jnp.clip(lag, 0, T - 1)]
    tz = jnp.where((lag >= 0)[:, :, None, None, None], tz, 0.0)
    tz_tab = tz.transpose(2, 0, 4, 1, 3).reshape(G, T * cch, T * cch)

    at = pw[T]
    a_tab = jnp.stack([
        jnp.concatenate([at.real, at.real], axis=-1),
        jnp.concatenate([-at.imag, at.imag], axis=-1),
        jnp.concatenate([at.imag, -at.imag], axis=-1),
    ], axis=1)
    return w_tab.astype(BF16), cp_tab.astype(BF16), tz_tab.astype(BF16), a_tab.astype(F32)


def _s5_kernel(u_ref, w_ref, cp_ref, tz_ref, a_ref, y_ref, c_ref, s_ref, *, gb, n_batch, nch):
    p2 = s_ref.shape[-1]
    for g in range(gb):
        c_ref[g] = jnp.dot(u_ref[g], w_ref[g], preferred_element_type=F32)

    a1 = [a_ref[g, 0:1, :] for g in range(gb)]
    a2 = [a_ref[g, 1:2, :] for g in range(gb)]
    a3 = [a_ref[g, 2:3, :] for g in range(gb)]
    chains = [(g, b) for g in range(gb) for b in range(n_batch)]

    def step(j, carry):
        out = []
        for (g, b), (x, xs) in zip(chains, carry):
            r = b * nch + j
            s_ref[g, pl.ds(r, 1), :] = x
            cc = c_ref[g, pl.ds(r, 1), :]
            out.append((a1[g] * x + a2[g] * xs + cc[:, :p2],
                        a1[g] * xs + a3[g] * x + cc[:, p2:]))
        return tuple(out)

    zero = jnp.zeros((1, p2), F32)
    lax.fori_loop(0, nch, step, tuple((zero, zero) for _ in chains))

    for g in range(gb):
        y = jnp.dot(s_ref[g].astype(BF16), cp_ref[g], preferred_element_type=F32)
        y = y + jnp.dot(u_ref[g], tz_ref[g], preferred_element_type=F32)
        y_ref[g] = y.astype(y_ref.dtype)


def _s5_scan(u, tables, batch, seq_len):
    w_tab, cp_tab, tz_tab, a_tab = tables
    N = u.shape[0]
    G = w_tab.shape[0]
    tc = w_tab.shape[1]
    p2 = cp_tab.shape[1]
    cch = tc // S5_CHUNK
    nch = seq_len // S5_CHUNK
    rows = batch * nch
    u_g = u.reshape(rows, S5_CHUNK, G, cch).transpose(2, 0, 1, 3).reshape(G, rows, tc)
    gb = _tile(G, 8)
    vmem = (2 * 2 * gb * rows * tc * 2 + 2 * gb * (tc * 2 * p2 + p2 * tc + tc * tc) * 2
            + gb * rows * 2 * p2 * 4 + gb * rows * p2 * 4 + 2 * rows * tc * 4)
    y_g = pl.pallas_call(
        functools.partial(_s5_kernel, gb=gb, n_batch=batch, nch=nch),
        grid=(G // gb,),
        in_specs=[
            pl.BlockSpec((gb, rows, tc), lambda i: (i, 0, 0)),
            pl.BlockSpec((gb, tc, 2 * p2), lambda i: (i, 0, 0)),
            pl.BlockSpec((gb, p2, tc), lambda i: (i, 0, 0)),
            pl.BlockSpec((gb, tc, tc), lambda i: (i, 0, 0)),
            pl.BlockSpec((gb, 3, p2), lambda i: (i, 0, 0)),
        ],
        out_specs=pl.BlockSpec((gb, rows, tc), lambda i: (i, 0, 0)),
        out_shape=jax.ShapeDtypeStruct((G, rows, tc), BF16),
        scratch_shapes=[
            pltpu.VMEM((gb, rows, 2 * p2), F32),
            pltpu.VMEM((gb, rows, p2), F32),
        ],
        compiler_params=_params(vmem, 1),
        name="s5_scan",
    )(u_g, w_tab, cp_tab, tz_tab, a_tab)
    return y_g.reshape(G, rows, S5_CHUNK, cch).transpose(1, 2, 0, 3).reshape(N, G * cch)


def _gelu_tanh(x):
    return 0.5 * x * (1.0 + jnp.tanh(math.sqrt(2.0 / math.pi) * (x + 0.044715 * (x * x * x))))


def _glu_kernel(y_ref, w_ref, o_ref):
    z = _gelu_tanh(y_ref[...].astype(F32))
    gate = jnp.dot(z.astype(BF16), w_ref[...], preferred_element_type=F32)
    o_ref[...] = (z * jax.nn.sigmoid(gate)).astype(o_ref.dtype)


def _glu(y, w_bf16):
    N, W = y.shape
    tm = _tile(N, 512)
    vmem = 2 * tm * W * 2 + 2 * W * W * 2 + 2 * tm * W * 2 + 4 * tm * W * 4
    return pl.pallas_call(
        _glu_kernel,
        grid=(N // tm,),
        in_specs=[pl.BlockSpec((tm, W), lambda i: (i, 0)),
                  pl.BlockSpec((W, W), lambda i: (0, 0))],
        out_specs=pl.BlockSpec((tm, W), lambda i: (i, 0)),
        out_shape=jax.ShapeDtypeStruct((N, W), BF16),
        compiler_params=_params(vmem, 1),
        name="glu",
    )(y, w_bf16)


def _merge_kernel(a_ref, s_ref, wa_ref, ws_ref, ga_ref, gs_ref, o_ref):
    ya = jnp.dot(a_ref[...], wa_ref[...], preferred_element_type=F32)
    ys = jnp.dot(s_ref[...], ws_ref[...], preferred_element_type=F32)
    o = jax.nn.sigmoid(ga_ref[...].astype(F32)) * ya + jax.nn.sigmoid(gs_ref[...].astype(F32)) * ys
    o_ref[...] = o.astype(o_ref.dtype)


def _merge(attn, ssm, wa_bf16, ws_bf16, proj, ga_col0, gs_col0):
    N, AW = attn.shape
    SW = ssm.shape[1]
    D = wa_bf16.shape[1]
    tm = _tile(N, 1024)
    tn = _tile(D, 512)
    assert ga_col0 % tn == 0 and gs_col0 % tn == 0
    ga_blk, gs_blk = ga_col0 // tn, gs_col0 // tn
    vmem = (2 * tm * (AW + SW) * 2 + 2 * (AW + SW) * tn * 2 + 3 * 2 * tm * tn * 2 + 3 * tm * tn * 4)
    return pl.pallas_call(
        _merge_kernel,
        grid=(N // tm, D // tn),
        in_specs=[
            pl.BlockSpec((tm, AW), lambda i, j: (i, 0)),
            pl.BlockSpec((tm, SW), lambda i, j: (i, 0)),
            pl.BlockSpec((AW, tn), lambda i, j: (0, j)),
            pl.BlockSpec((SW, tn), lambda i, j: (0, j)),
            pl.BlockSpec((tm, tn), lambda i, j: (i, ga_blk + j)),
            pl.BlockSpec((tm, tn), lambda i, j: (i, gs_blk + j)),
        ],
        out_specs=pl.BlockSpec((tm, tn), lambda i, j: (i, j)),
        out_shape=jax.ShapeDtypeStruct((N, D), BF16),
        compiler_params=_params(vmem, 2),
        name="merge",
    )(attn, ssm, wa_bf16, ws_bf16, proj, proj)


def _out_proj_kernel(m_ref, w_ref, x_ref, g_ref, o_ref):
    o_ref[...] = x_ref[...] + g_ref[...] * jnp.dot(m_ref[...], w_ref[...], preferred_element_type=F32)


def _out_proj(merged, w_bf16, x2d, mod3, seq_len):
    N, D = x2d.shape
    K = merged.shape[1]
    tm = _tile(seq_len, 1024)
    tn = _tile(D, 512)
    tpb = seq_len // tm
    g1_blk = 2 * (D // tn)
    vmem = 2 * tm * K * 2 + 2 * K * tn * 2 + 4 * tm * tn * 4 + tm * tn * 4
    return pl.pallas_call(
        _out_proj_kernel,
        grid=(N // tm, D // tn),
        in_specs=[
            pl.BlockSpec((tm, K), lambda i, j: (i, 0)),
            pl.BlockSpec((K, tn), lambda i, j: (0, j)),
            pl.BlockSpec((tm, tn), lambda i, j: (i, j)),
            pl.BlockSpec((None, 1, tn), lambda i, j: (i // tpb, 0, g1_blk + j)),
        ],
        out_specs=pl.BlockSpec((tm, tn), lambda i, j: (i, j)),
        out_shape=jax.ShapeDtypeStruct((N, D), F32),
        compiler_params=_params(vmem, 2),
        name="out_proj",
    )(merged, w_bf16, x2d, mod3)


def _ffn_up_kernel(x_ref, halo_ref, g_ref, sc_ref, sh_ref, wa_ref, wg_ref, cwa_ref, cwg_ref,
                   o_ref, h_ref, ua_ref, ug_ref, *, tm, halo, seq_len, taps):
    i = pl.program_id(0)

    @pl.when(pl.program_id(1) == 0)
    def _():
        _adaln_rows(x_ref, g_ref, sc_ref, sh_ref, h_ref, tm, halo)
        _adaln_rows(halo_ref, g_ref, sc_ref, sh_ref, h_ref, halo, 0)

        @pl.when((i * tm) % seq_len == 0)
        def _():
            h_ref[0:halo, :] = jnp.zeros((halo, h_ref.shape[1]), BF16)

    h = h_ref[...]
    ua_ref[...] = jnp.dot(h, wa_ref[...], preferred_element_type=F32)
    ug_ref[...] = jnp.dot(h, wg_ref[...], preferred_element_type=F32)

    def conv(u_ref, cw_ref):
        out = cw_ref[taps:taps + 1, :]
        for j in range(taps):
            out = out + cw_ref[j:j + 1, :] * u_ref[pl.ds(halo - (taps - 1) + j, tm), :]
        return out

    a = conv(ua_ref, cwa_ref)
    g = conv(ug_ref, cwg_ref)
    o_ref[...] = (g * jax.nn.sigmoid(g) * a).astype(o_ref.dtype)


def _ffn_up(x1, norm_g, mod3, w_up_bf16, conv_tab, seq_len, ffp, taps):
    N, D = x1.shape
    tm = _tile(seq_len, 1024)
    tn = _tile(ffp, 512)
    halo = BF16_SUBLANES
    assert taps - 1 <= halo and tm % halo == 0
    tpb = seq_len // tm
    nj = ffp // tn
    hb = tm // halo
    vmem = (tm * D * 4 + 2 * halo * D * 4 + (tm + halo) * D * 2 + 2 * 2 * D * tn * 2
            + 2 * tm * tn * 2 + 2 * (tm + halo) * tn * 4 + 4 * tm * tn * 4)
    return pl.pallas_call(
        functools.partial(_ffn_up_kernel, tm=tm, halo=halo, seq_len=seq_len, taps=taps),
        grid=(N // tm, nj),
        in_specs=[
            pl.BlockSpec((tm, D), lambda i, j: (i, 0), pipeline_mode=pl.Buffered(1)),
            pl.BlockSpec((halo, D), lambda i, j: (jnp.maximum(i * hb - 1, 0), 0)),
            pl.BlockSpec((1, D), lambda i, j: (0, 0)),
            pl.BlockSpec((None, 1, D), lambda i, j: (i // tpb, 0, 4)),
            pl.BlockSpec((None, 1, D), lambda i, j: (i // tpb, 0, 3)),
            pl.BlockSpec((D, tn), lambda i, j: (0, j)),
            pl.BlockSpec((D, tn), lambda i, j: (0, nj + j)),
            pl.BlockSpec((8, tn), lambda i, j: (0, j)),
            pl.BlockSpec((8, tn), lambda i, j: (0, nj + j)),
        ],
        out_specs=pl.BlockSpec((tm, tn), lambda i, j: (i, j)),
        out_shape=jax.ShapeDtypeStruct((N, ffp), BF16),
        scratch_shapes=[
            pltpu.VMEM((tm + halo, D), BF16),
            pltpu.VMEM((tm + halo, tn), F32),
            pltpu.VMEM((tm + halo, tn), F32),
        ],
        compiler_params=_params(vmem, 2),
        name="ffn_up",
    )(x1, x1, norm_g.reshape(1, D), mod3, mod3, w_up_bf16, w_up_bf16, conv_tab, conv_tab)


def _ffn_down_kernel(a_ref, w_ref, x_ref, g_ref, fg_ref, o_ref, *, tm, nk, final_norm):
    k = pl.program_id(1)

    @pl.when(k == 0)
    def _():
        o_ref[...] = jnp.zeros(o_ref.shape, F32)

    o_ref[...] += jnp.dot(a_ref[...], w_ref[...], preferred_element_type=F32)

    @pl.when(k == nk - 1)
    def _():
        gate = g_ref[...]
        fg = fg_ref[...]
        rows = min(NORM_ROWS, tm)

        def body(r, carry):
            sl = pl.ds(r * rows, rows)
            x2 = x_ref[sl, :] + gate * o_ref[sl, :]
            if final_norm:
                ms = jnp.mean(x2 * x2, axis=-1, keepdims=True)
                x2 = (x2 * lax.rsqrt(ms + EPS)) * fg
            o_ref[sl, :] = x2
            return carry

        lax.fori_loop(0, tm // rows, body, 0)


def _ffn_down(act, w_bf16, x1, mod3, final_g, seq_len, final_norm):
    N, D = x1.shape
    K = act.shape[1]
    tm = _tile(seq_len, 512)
    tk = _tile(K, 1024)
    tpb = seq_len // tm
    nk = K // tk
    vmem = 2 * tm * tk * 2 + 2 * tk * D * 2 + tm * D * 4 + 2 * tm * D * 4 + tm * D * 4
    return pl.pallas_call(
        functools.partial(_ffn_down_kernel, tm=tm, nk=nk, final_norm=final_norm),
        grid=(N // tm, nk),
        in_specs=[
            pl.BlockSpec((tm, tk), lambda i, k: (i, k)),
            pl.BlockSpec((tk, D), lambda i, k: (k, 0)),
            pl.BlockSpec((tm, D), lambda i, k: (i, 0), pipeline_mode=pl.Buffered(1)),
            pl.BlockSpec((None, 1, D), lambda i, k: (i // tpb, 0, 5)),
            pl.BlockSpec((1, D), lambda i, k: (0, 0)),
        ],
        out_specs=pl.BlockSpec((tm, D), lambda i, k: (i, 0)),
        out_shape=jax.ShapeDtypeStruct((N, D), F32),
        compiler_params=_params(vmem, 2),
        name="ffn_down",
    )(act, w_bf16, x1, mod3, final_g.reshape(1, D))


def _round_up(n, m):
    return (n + m - 1) // m * m


def kernel(x, c, ada_w, ada_b, norm1_g, w_in, lq1, lk1, lq2, lk2, subln_g, a_re, a_im, b_re, b_im,
           c_re, c_im, d_skip, log_dt, w_glu, w_attn_br, w_ssm_br, w_out, norm2_g, w_up, conv_w,
           conv_b, w_down, final_g):
    B, L, D = x.shape
    depth = ada_w.shape[0]
    d = lq1.shape[-1]
    aw = w_attn_br.shape[1]
    sw = w_ssm_br.shape[1]
    n_heads = aw // (2 * d)
    ff = w_down.shape[1]
    taps = conv_w.shape[1]
    ffp = _round_up(ff, 1024) if ff > 1024 else ff
    u_col0 = 3 * aw
    ga_col0 = u_col0 + sw
    gs_col0 = ga_col0 + D

    xc = x.reshape(B * L, D)
    for l in range(depth):
        lam_init = 0.8 - 0.6 * math.exp(-0.3 * l)
        lam = (jnp.exp(jnp.sum(lq1[l].astype(F32) * lk1[l].astype(F32)))
               - jnp.exp(jnp.sum(lq2[l].astype(F32) * lk2[l].astype(F32))) + lam_init)
        mod3 = _ada_mod(c, ada_w[l], ada_b[l])

        proj = _in_proj(xc, norm1_g[l], mod3, w_in[l].astype(BF16), L)
        attn = _diff_attention(proj, lam, subln_g[l], B, L, n_heads, d, lam_init)
        tables = _s5_tables(a_re[l], a_im[l], b_re[l], b_im[l], c_re[l], c_im[l], d_skip[l], log_dt[l])
        y = _s5_scan(proj[:, u_col0:u_col0 + sw], tables, B, L)
        ssm = _glu(y, w_glu[l].astype(BF16))
        merged = _merge(attn, ssm, w_attn_br[l].astype(BF16), w_ssm_br[l].astype(BF16),
                        proj, ga_col0, gs_col0)
        x1 = _out_proj(merged, w_out[l].astype(BF16), xc, mod3, L)

        pad = ffp - ff
        w_up_p = jnp.concatenate([jnp.pad(w_up[l][:, :ff], ((0, 0), (0, pad))),
                                  jnp.pad(w_up[l][:, ff:], ((0, 0), (0, pad)))], axis=1).astype(BF16)
        conv_rows = jnp.concatenate([conv_w[l], conv_b[l][None]], axis=0).astype(F32)
        conv_tab = jnp.concatenate([jnp.pad(conv_rows[:, :ff], ((0, 8 - taps - 1), (0, pad))),
                                    jnp.pad(conv_rows[:, ff:], ((0, 8 - taps - 1), (0, pad)))], axis=1)
        w_down_p = jnp.pad(w_down[l], ((0, pad), (0, 0))).astype(BF16)
        act = _ffn_up(x1, norm2_g[l], mod3, w_up_p, conv_tab, L, ffp, taps)
        xc = _ffn_down(act, w_down_p, x1, mod3, final_g, L, final_norm=(l == depth - 1))
    return xc.reshape(B, L, D)
```

```python
import functools
import math

import jax
import jax.numpy as jnp
from jax import lax
from jax.experimental import pallas as pl
from jax.experimental.pallas import tpu as pltpu

F32 = jnp.float32
BF16 = jnp.bfloat16
EPS = 1e-6
NEG_BIG = -1e30
LOG2E = 1.0 / math.log(2.0)

V7X_VMEM_BYTES = 64 * 1024 * 1024
V7X_LANES = 128
V7X_MXU_DIM = 256
BF16_SUBLANES = 16
S5_CHUNK = 16
NORM_ROWS = 32
ATTN_BLOCK = 512
CONV_ROWS = 64


def _vmem_limit(block_bytes):
    return int(min(block_bytes + 12 * 1024 * 1024, V7X_VMEM_BYTES - 1024 * 1024))


def _tile(dim, want):
    t = min(dim, want)
    assert dim % t == 0, (dim, want)
    return t


def _params(vmem_bytes, n_grid):
    return pltpu.CompilerParams(
        dimension_semantics=("arbitrary",) * n_grid,
        vmem_limit_bytes=_vmem_limit(vmem_bytes),
    )


def _ada_kernel(c_ref, w_ref, b_ref, o_ref):
    w = w_ref[...].astype(BF16)
    o_ref[...] = jnp.dot(c_ref[...], w, preferred_element_type=F32) + b_ref[...]


def _ada_mod(c, ada_w, ada_b):
    B, D = c.shape
    n_out = ada_w.shape[1]
    rows = 8
    c_pad = jnp.zeros((rows, D), BF16).at[:B].set(c.astype(BF16))
    tn = _tile(n_out, 512)
    out = pl.pallas_call(
        _ada_kernel,
        grid=(n_out // tn,),
        in_specs=[
            pl.BlockSpec((rows, D), lambda j: (0, 0)),
            pl.BlockSpec((D, tn), lambda j: (0, j)),
            pl.BlockSpec((1, tn), lambda j: (0, j)),
        ],
        out_specs=pl.BlockSpec((rows, tn), lambda j: (0, j)),
        out_shape=jax.ShapeDtypeStruct((rows, n_out), F32),
        compiler_params=_params(2 * D * tn * 4 + D * tn * 2, 1),
        name="ada_mod",
    )(c_pad, ada_w, ada_b.reshape(1, n_out))
    return out[:B].reshape(B, 1, n_out)


def _adaln_rows(x_ref, g_ref, sc_ref, sh_ref, h_ref, n_rows, dst_row0):
    g = g_ref[...]
    one_sc = 1.0 + sc_ref[...]
    sh = sh_ref[...]
    rows = min(NORM_ROWS, n_rows)

    def body(r, carry):
        x = x_ref[pl.ds(r * rows, rows), :]
        ms = jnp.mean(x * x, axis=-1, keepdims=True)
        y = (x * lax.rsqrt(ms + EPS)) * g
        h_ref[pl.ds(dst_row0 + r * rows, rows), :] = (y * one_sc + sh).astype(BF16)
        return carry

    lax.fori_loop(0, n_rows // rows, body, 0)


def _in_proj_kernel(x_ref, g_ref, sc_ref, sh_ref, w_ref, o_ref, h_ref, *, tm):
    @pl.when(pl.program_id(1) == 0)
    def _():
        _adaln_rows(x_ref, g_ref, sc_ref, sh_ref, h_ref, tm, 0)

    o_ref[...] = jnp.dot(h_ref[...], w_ref[...], preferred_element_type=F32).astype(o_ref.dtype)


def _in_proj(x2d, norm_g, mod3, w_bf16, seq_len):
    N, D = x2d.shape
    n_out = w_bf16.shape[1]
    tm = _tile(seq_len, 1024)
    tn = _tile(n_out, 512)
    tpb = seq_len // tm
    vmem = 2 * tm * D * 4 + tm * D * 2 + 2 * D * tn * 2 + 2 * tm * tn * 2 + tm * tn * 4
    return pl.pallas_call(
        functools.partial(_in_proj_kernel, tm=tm),
        grid=(N // tm, n_out // tn),
        in_specs=[
            pl.BlockSpec((tm, D), lambda i, j: (i, 0)),
            pl.BlockSpec((1, D), lambda i, j: (0, 0)),
            pl.BlockSpec((None, 1, D), lambda i, j: (i // tpb, 0, 1)),
            pl.BlockSpec((None, 1, D), lambda i, j: (i // tpb, 0, 0)),
            pl.BlockSpec((D, tn), lambda i, j: (0, j)),
        ],
        out_specs=pl.BlockSpec((tm, tn), lambda i, j: (i, j)),
        out_shape=jax.ShapeDtypeStruct((N, n_out), BF16),
        scratch_shapes=[pltpu.VMEM((tm, D), BF16)],
        compiler_params=_params(vmem, 2),
        name="in_proj",
    )(x2d, norm_g.reshape(1, D), mod3, mod3, w_bf16)


def _attn_kernel(slope_ref, lam_ref, q_ref, k_ref, v_ref, g_ref, o_ref,
                 qa_ref, ek_ref, vt_ref, s_ref, m_ref, l_ref, acc_ref, *, tq, d, out_scale):
    h = pl.program_id(1)
    qi = pl.program_id(2)
    lam = lam_ref[0]
    tk = tq
    hw = 2 * d
    nkb = vt_ref.shape[0]
    slope2 = slope_ref[h] * LOG2E

    @pl.when(qi == 0)
    def _():
        for c in range(nkb):
            vt_ref[c] = v_ref[c * tk:(c + 1) * tk, :].astype(F32).T.astype(BF16)
        jj = lax.broadcasted_iota(jnp.int32, (tk, hw), 0)
        ln = lax.broadcasted_iota(jnp.int32, (tk, hw), 1)
        hi = ((jj // 32) * 32).astype(F32)
        lo = (jj % 32).astype(F32)
        ek_ref[...] = jnp.where(ln < 3, hi, jnp.where(ln < 6, lo, 0.0)).astype(BF16)

    qf = q_ref[...].astype(F32) * (d ** -0.5 * LOG2E)
    lane = lax.broadcasted_iota(jnp.int32, (tq, hw), 1)
    qa_ref[0:tq, 0:hw] = jnp.where(lane < d, qf, 0.0).astype(BF16)
    qa_ref[tq:2 * tq, 0:hw] = jnp.where(lane >= d, qf, 0.0).astype(BF16)
    sl = jnp.full((1, hw), slope2, F32)
    pa = sl.astype(BF16).astype(F32)
    ra = sl - pa
    pb = ra.astype(BF16).astype(F32)
    pc = (ra - pb).astype(BF16).astype(F32)
    l1 = lax.broadcasted_iota(jnp.int32, (1, hw), 1)
    piece = jnp.where(l1 % 3 == 0, pa, jnp.where(l1 % 3 == 1, pb, pc))
    eq = jnp.where(l1 < 6, piece, 0.0)
    qa_ref[:, hw:2 * hw] = jnp.broadcast_to(eq, (2 * tq, hw)).astype(BF16)

    m_ref[...] = jnp.full(m_ref.shape, NEG_BIG, F32)
    l_ref[...] = jnp.zeros(l_ref.shape, F32)
    acc_ref[...] = jnp.zeros(acc_ref.shape, F32)

    def scores(kb, slot):
        k = k_ref[pl.ds(pl.multiple_of(kb * tk, tk), tk), :]
        ka = jnp.concatenate([k, ek_ref[...]], axis=1)
        s_ref[slot] = lax.dot_general(ka, qa_ref[...], (((1,), (1,)), ((), ())),
                                      preferred_element_type=F32)

    def softmax_pv(kb, slot, masked):
        s = s_ref[slot]
        if masked:
            key = lax.broadcasted_iota(jnp.int32, (tk, 2 * tq), 0)
            qry = lax.broadcasted_iota(jnp.int32, (tk, 2 * tq), 1)
            qry = jnp.where(qry >= tq, qry - tq, qry)
            s = jnp.where(key <= qry, s, NEG_BIG)
        off = slope2 * ((kb - qi) * tk).astype(F32)
        m_old = m_ref[...]
        m_new = jnp.maximum(m_old, jnp.max(s, axis=0, keepdims=True) + off)
        alpha = jnp.exp2(m_old - m_new)
        p = jnp.exp2(s - (m_new - off))
        l_ref[...] = alpha * l_ref[...] + jnp.sum(p, axis=0, keepdims=True)
        pv = jnp.dot(vt_ref[kb], p.astype(BF16), preferred_element_type=F32)
        acc_ref[...] = alpha * acc_ref[...] + pv
        m_ref[...] = m_new

    scores(0, 0)

    def pair(i, carry):
        kb = 2 * i
        scores(kb + 1, 1)
        softmax_pv(kb, 0, False)
        scores(kb + 2, 0)
        softmax_pv(kb + 1, 1, False)
        return carry

    lax.fori_loop(0, qi // 2, pair, 0)

    @pl.when(qi % 2 == 0)
    def _():
        softmax_pv(qi, 0, True)

    @pl.when(qi % 2 == 1)
    def _():
        scores(qi, 1)
        softmax_pv(qi - 1, 0, False)
        softmax_pv(qi, 1, True)

    inv_l = 1.0 / l_ref[...]
    o = acc_ref[:, 0:tq] * inv_l[:, 0:tq] - lam * (acc_ref[:, tq:2 * tq] * inv_l[:, tq:2 * tq])
    ms = jnp.mean(o * o, axis=0, keepdims=True)
    y = (o * lax.rsqrt(ms + EPS)).T
    o_ref[...] = (y * g_ref[...] * out_scale).astype(o_ref.dtype)


def _diff_attention(proj, lam, subln_g, batch, seq_len, n_heads, d, lam_init):
    N = proj.shape[0]
    hw = 2 * d
    assert hw == V7X_LANES
    tq = _tile(seq_len, ATTN_BLOCK)
    assert tq % 32 == 0 and tq <= 32 * 256
    nq = seq_len // tq
    slopes = jnp.exp2(-8.0 * jnp.arange(1, n_heads + 1, dtype=F32) / n_heads)
    vmem = (2 * 2 * tq * hw * 2 + 2 * 2 * seq_len * hw * 2 + seq_len * hw * 2
            + 2 * tq * 2 * hw * 2 + tq * hw * 2 + 2 * tq * 2 * tq * 4
            + hw * 2 * tq * 4 + 3 * tq * 2 * tq * 4)
    grid_spec = pltpu.PrefetchScalarGridSpec(
        num_scalar_prefetch=2,
        grid=(batch, n_heads, nq),
        in_specs=[
            pl.BlockSpec((tq, hw), lambda b, h, qi, *_: (b * nq + qi, h)),
            pl.BlockSpec((seq_len, hw), lambda b, h, qi, *_: (b, n_heads + h)),
            pl.BlockSpec((seq_len, hw), lambda b, h, qi, *_: (b, 2 * n_heads + h)),
            pl.BlockSpec((1, hw), lambda b, h, qi, *_: (0, 0)),
        ],
        out_specs=pl.BlockSpec((tq, hw), lambda b, h, qi, *_: (b * nq + qi, h)),
        scratch_shapes=[
            pltpu.VMEM((2 * tq, 2 * hw), BF16),
            pltpu.VMEM((tq, hw), BF16),
            pltpu.VMEM((seq_len // tq, hw, tq), BF16),
            pltpu.VMEM((2, tq, 2 * tq), F32),
            pltpu.VMEM((1, 2 * tq), F32),
            pltpu.VMEM((1, 2 * tq), F32),
            pltpu.VMEM((hw, 2 * tq), F32),
        ],
    )
    return pl.pallas_call(
        functools.partial(_attn_kernel, tq=tq, d=d, out_scale=1.0 - lam_init),
        grid_spec=grid_spec,
        out_shape=jax.ShapeDtypeStruct((N, n_heads * hw), BF16),
        compiler_params=_params(vmem, 3),
        name="diff_attn",
    )(slopes, lam.reshape(1), proj, proj, proj, subln_g.reshape(1, hw))


def _s5_tables(a_re, a_im, b_re, b_im, c_re, c_im, d_skip, log_dt):
    hp = lax.Precision.HIGHEST
    G, P = a_re.shape
    cch = b_re.shape[-1]
    T = S5_CHUNK
    lam = lax.complex(a_re.astype(F32), a_im.astype(F32))
    dt = jnp.exp(log_dt.astype(F32))[:, None]
    lam_dt = lam * dt
    a_bar = jnp.exp(lam_dt)
    b_bar = ((a_bar - 1.0) / lam)[..., None] * lax.complex(b_re.astype(F32), b_im.astype(F32))
    c_mat = lax.complex(c_re.astype(F32), c_im.astype(F32))
    pw = jnp.exp(lam_dt[None] * jnp.arange(0, T + 1, dtype=F32)[:, None, None])

    w_c = jnp.einsum('sgp,gpc->gscp', pw[T - 1 - jnp.arange(T)], b_bar).reshape(G, T * cch, P)
    w_tab = jnp.concatenate([w_c.real, w_c.imag, w_c.imag, w_c.real], axis=-1)

    m_c = jnp.einsum('gcp,tgp->gptc', c_mat, pw[1:T + 1]).reshape(G, P, T * cch)
    cp_tab = jnp.concatenate([m_c.real, -m_c.imag], axis=1)

    cr, ci = c_mat.real, c_mat.imag
    ab = pw[:T][:, :, :, None] * b_bar[None]
    k_tab = (jnp.einsum('gcp,kgpd->kgcd', cr, ab.real, precision=hp)
             - jnp.einsum('gcp,kgpd->kgcd', ci, ab.imag, precision=hp))
    k_tab = k_tab.at[0].add(jnp.eye(cch, dtype=F32)[None] * d_skip.astype(F32).reshape(G, cch, 1))
    lag = jnp.arange(T)[None, :] - jnp.arange(T)[:, None]
    tz = k_tab[jnp.clip(lag, 0, T - 1)]
    tz = jnp.where((lag >= 0)[:, :, None, None, None], tz, 0.0)
    tz_tab = tz.transpose(2, 0, 4, 1, 3).reshape(G, T * cch, T * cch)

    at = pw[T]
    a_tab = jnp.stack([
        jnp.concatenate([at.real, at.real], axis=-1),
        jnp.concatenate([-at.imag, at.imag], axis=-1),
        jnp.concatenate([at.imag, -at.imag], axis=-1),
    ], axis=1)
    return w_tab.astype(BF16), cp_tab.astype(BF16), tz_tab.astype(BF16), a_tab.astype(F32)


def _s5_kernel(u_ref, w_ref, cp_ref, tz_ref, a_ref, y_ref, xf_ref, ug_ref, c_ref, s_ref,
               *, gb, cch, n_batch, nch, rb):
    T = S5_CHUNK
    p2 = s_ref.shape[-1]
    lanes = gb * cch
    halves = T // gb
    rows = n_batch * nch
    gran = lax.broadcasted_iota(jnp.int32, (1, lanes), 1) // cch

    def gather_rows(t, carry):
        tok0 = pl.multiple_of(t * (rb * T), rb * T)
        r0 = pl.multiple_of(t * rb, rb)
        xf_ref[...] = u_ref[pl.ds(tok0, rb * T), :].astype(F32)
        x = [xf_ref[pl.ds(s, rb, stride=T), :] for s in range(T)]
        out = [[None] * halves for _ in range(gb)]
        for hf in range(halves):
            for k in range(gb):
                z = jnp.zeros((rb, lanes), F32)
                for s8 in range(gb):
                    z = jnp.where(gran == (s8 - k) % gb, x[hf * gb + s8], z)
                if k:
                    z = pltpu.roll(z, k * cch, axis=1)
                for g in range(gb):
                    prev = out[g][hf] if out[g][hf] is not None else jnp.zeros((rb, lanes), F32)
                    out[g][hf] = jnp.where(gran == (g + k) % gb, z, prev)
        for g in range(gb):
            ug_ref[g, pl.ds(r0, rb), :] = jnp.concatenate(out[g], axis=1).astype(BF16)
        return carry

    lax.fori_loop(0, rows // rb, gather_rows, 0)

    for g in range(gb):
        c_ref[g] = jnp.dot(ug_ref[g], w_ref[g], preferred_element_type=F32)

    a1 = [a_ref[g, 0:1, :] for g in range(gb)]
    a2 = [a_ref[g, 1:2, :] for g in range(gb)]
    a3 = [a_ref[g, 2:3, :] for g in range(gb)]
    chains = [(g, b) for g in range(gb) for b in range(n_batch)]

    def step(j, carry):
        out = []
        for (g, b), (x, xs) in zip(chains, carry):
            r = b * nch + j
            s_ref[g, pl.ds(r, 1), :] = x
            cc = c_ref[g, pl.ds(r, 1), :]
            out.append((a1[g] * x + a2[g] * xs + cc[:, :p2],
                        a1[g] * xs + a3[g] * x + cc[:, p2:]))
        return tuple(out)

    zero = jnp.zeros((1, p2), F32)
    lax.fori_loop(0, nch, step, tuple((zero, zero) for _ in chains))

    for g in range(gb):
        y = jnp.dot(s_ref[g].astype(BF16), cp_ref[g], preferred_element_type=F32)
        c_ref[g] = y + jnp.dot(ug_ref[g], tz_ref[g], preferred_element_type=F32)

    def scatter_rows(t, carry):
        tok0 = pl.multiple_of(t * (rb * T), rb * T)
        r0 = pl.multiple_of(t * rb, rb)
        yg = [[c_ref[g, pl.ds(r0, rb), hf * lanes:(hf + 1) * lanes] for hf in range(halves)]
              for g in range(gb)]
        y = [None] * T
        for hf in range(halves):
            for k in range(gb):
                z = jnp.zeros((rb, lanes), F32)
                for g in range(gb):
                    z = jnp.where(gran == (g + k) % gb, yg[g][hf], z)
                if k:
                    z = pltpu.roll(z, (gb - k) * cch, axis=1)
                for s8 in range(gb):
                    s = hf * gb + s8
                    prev = y[s] if y[s] is not None else jnp.zeros((rb, lanes), F32)
                    y[s] = jnp.where(gran == (s8 - k) % gb, z, prev)
        for s in range(T):
            xf_ref[pl.ds(s, rb, stride=T), :] = y[s]
        y_ref[pl.ds(tok0, rb * T), :] = xf_ref[...].astype(y_ref.dtype)
        return carry

    lax.fori_loop(0, rows // rb, scatter_rows, 0)


def _s5_scan(proj, u_col0, tables, batch, seq_len):
    w_tab, cp_tab, tz_tab, a_tab = tables
    N = proj.shape[0]
    G = w_tab.shape[0]
    tc = w_tab.shape[1]
    p2 = cp_tab.shape[1]
    cch = tc // S5_CHUNK
    nch = seq_len // S5_CHUNK
    rows = batch * nch
    gb = V7X_LANES // cch
    assert G % gb == 0 and S5_CHUNK % gb == 0 and u_col0 % V7X_LANES == 0
    assert 2 * p2 == tc
    rb = _tile(rows, 128)
    u_blk0 = u_col0 // V7X_LANES
    vmem = (2 * 2 * N * V7X_LANES * 2 + 2 * gb * (tc * 2 * p2 + p2 * tc + tc * tc) * 2
            + rb * S5_CHUNK * V7X_LANES * 4 + gb * rows * tc * 2
            + gb * rows * 2 * p2 * 4 + gb * rows * p2 * 4 + 2 * rows * tc * 4)
    return pl.pallas_call(
        functools.partial(_s5_kernel, gb=gb, cch=cch, n_batch=batch, nch=nch, rb=rb),
        grid=(G // gb,),
        in_specs=[
            pl.BlockSpec((N, V7X_LANES), lambda i: (0, u_blk0 + i)),
            pl.BlockSpec((gb, tc, 2 * p2), lambda i: (i, 0, 0)),
            pl.BlockSpec((gb, p2, tc), lambda i: (i, 0, 0)),
            pl.BlockSpec((gb, tc, tc), lambda i: (i, 0, 0)),
            pl.BlockSpec((gb, 3, p2), lambda i: (i, 0, 0)),
        ],
        out_specs=pl.BlockSpec((N, V7X_LANES), lambda i: (0, i)),
        out_shape=jax.ShapeDtypeStruct((N, G * cch), BF16),
        scratch_shapes=[
            pltpu.VMEM((rb * S5_CHUNK, V7X_LANES), F32),
            pltpu.VMEM((gb, rows, tc), BF16),
            pltpu.VMEM((gb, rows, 2 * p2), F32),
            pltpu.VMEM((gb, rows, p2), F32),
        ],
        compiler_params=_params(vmem, 1),
        name="s5_scan",
    )(proj, w_tab, cp_tab, tz_tab, a_tab)


def _gelu_tanh(x):
    return 0.5 * x * (1.0 + jnp.tanh(math.sqrt(2.0 / math.pi) * (x + 0.044715 * (x * x * x))))


def _glu_kernel(y_ref, w_ref, o_ref):
    z = _gelu_tanh(y_ref[...].astype(F32))
    gate = jnp.dot(z.astype(BF16), w_ref[...], preferred_element_type=F32)
    o_ref[...] = (z * jax.nn.sigmoid(gate)).astype(o_ref.dtype)


def _glu(y, w_bf16):
    N, W = y.shape
    tm = _tile(N, 512)
    vmem = 2 * tm * W * 2 + 2 * W * W * 2 + 2 * tm * W * 2 + 4 * tm * W * 4
    return pl.pallas_call(
        _glu_kernel,
        grid=(N // tm,),
        in_specs=[pl.BlockSpec((tm, W), lambda i: (i, 0)),
                  pl.BlockSpec((W, W), lambda i: (0, 0))],
        out_specs=pl.BlockSpec((tm, W), lambda i: (i, 0)),
        out_shape=jax.ShapeDtypeStruct((N, W), BF16),
        compiler_params=_params(vmem, 1),
        name="glu",
    )(y, w_bf16)


def _merge_kernel(a_ref, s_ref, wa_ref, ws_ref, ga_ref, gs_ref, o_ref):
    ya = jnp.dot(a_ref[...], wa_ref[...], preferred_element_type=F32)
    ys = jnp.dot(s_ref[...], ws_ref[...], preferred_element_type=F32)
    o = jax.nn.sigmoid(ga_ref[...].astype(F32)) * ya + jax.nn.sigmoid(gs_ref[...].astype(F32)) * ys
    o_ref[...] = o.astype(o_ref.dtype)


def _merge(attn, ssm, wa_bf16, ws_bf16, proj, ga_col0, gs_col0):
    N, AW = attn.shape
    SW = ssm.shape[1]
    D = wa_bf16.shape[1]
    tm = _tile(N, 1024)
    tn = _tile(D, 512)
    assert ga_col0 % tn == 0 and gs_col0 % tn == 0
    ga_blk, gs_blk = ga_col0 // tn, gs_col0 // tn
    vmem = (2 * tm * (AW + SW) * 2 + 2 * (AW + SW) * tn * 2 + 3 * 2 * tm * tn * 2 + 3 * tm * tn * 4)
    return pl.pallas_call(
        _merge_kernel,
        grid=(N // tm, D // tn),
        in_specs=[
            pl.BlockSpec((tm, AW), lambda i, j: (i, 0)),
            pl.BlockSpec((tm, SW), lambda i, j: (i, 0)),
            pl.BlockSpec((AW, tn), lambda i, j: (0, j)),
            pl.BlockSpec((SW, tn), lambda i, j: (0, j)),
            pl.BlockSpec((tm, tn), lambda i, j: (i, ga_blk + j)),
            pl.BlockSpec((tm, tn), lambda i, j: (i, gs_blk + j)),
        ],
        out_specs=pl.BlockSpec((tm, tn), lambda i, j: (i, j)),
        out_shape=jax.ShapeDtypeStruct((N, D), BF16),
        compiler_params=_params(vmem, 2),
        name="merge",
    )(attn, ssm, wa_bf16, ws_bf16, proj, proj)


def _out_proj_kernel(m_ref, w_ref, x_ref, g_ref, o_ref):
    o_ref[...] = x_ref[...] + g_ref[...] * jnp.dot(m_ref[...], w_ref[...], preferred_element_type=F32)


def _out_proj(merged, w_bf16, x2d, mod3, seq_len):
    N, D = x2d.shape
    K = merged.shape[1]
    tm = _tile(seq_len, 1024)
    tn = _tile(D, 512)
    tpb = seq_len // tm
    g1_blk = 2 * (D // tn)
    vmem = 2 * tm * K * 2 + 2 * K * tn * 2 + 4 * tm * tn * 4 + tm * tn * 4
    return pl.pallas_call(
        _out_proj_kernel,
        grid=(N // tm, D // tn),
        in_specs=[
            pl.BlockSpec((tm, K), lambda i, j: (i, 0)),
            pl.BlockSpec((K, tn), lambda i, j: (0, j)),
            pl.BlockSpec((tm, tn), lambda i, j: (i, j)),
            pl.BlockSpec((None, 1, tn), lambda i, j: (i // tpb, 0, g1_blk + j)),
        ],
        out_specs=pl.BlockSpec((tm, tn), lambda i, j: (i, j)),
        out_shape=jax.ShapeDtypeStruct((N, D), F32),
        compiler_params=_params(vmem, 2),
        name="out_proj",
    )(merged, w_bf16, x2d, mod3)


def _ffn_up_kernel(x_ref, halo_ref, g_ref, sc_ref, sh_ref, wa_ref, wg_ref, cwa_ref, cwg_ref,
                   o_ref, h_ref, u0_ref, u1_ref, *, tm, halo, seq_len, taps, nj):
    i = pl.program_id(0)
    j = pl.program_id(1)

    @pl.when(j == 0)
    def _():
        _adaln_rows(x_ref, g_ref, sc_ref, sh_ref, h_ref, tm, halo)
        _adaln_rows(halo_ref, g_ref, sc_ref, sh_ref, h_ref, halo, 0)

        @pl.when((i * tm) % seq_len == 0)
        def _():
            h_ref[0:halo, :] = jnp.zeros((halo, h_ref.shape[1]), BF16)

    u_refs = (u0_ref, u1_ref)

    def matmuls(slot):
        h = h_ref[...]
        u_refs[slot][0] = jnp.dot(h, wa_ref[...], preferred_element_type=F32)
        u_refs[slot][1] = jnp.dot(h, wg_ref[...], preferred_element_type=F32)

    rc = min(CONV_ROWS, tm)
    lead = 8

    def conv(slot, which, cw_ref, r0):
        u = u_refs[slot][which, halo + r0 - lead:halo + r0 + rc, :]
        out = cw_ref[taps:taps + 1, :] + cw_ref[taps - 1:taps, :] * u[lead:, :]
        for back in range(1, taps):
            out = out + cw_ref[taps - 1 - back:taps - back, :] * pltpu.roll(u, back, axis=0)[lead:, :]
        return out

    def epilogue(slot):
        for r0 in range(0, tm, rc):
            a = conv(slot, 0, cwa_ref, r0)
            g = conv(slot, 1, cwg_ref, r0)
            o_ref[r0:r0 + rc, :] = (g * jax.nn.sigmoid(g) * a).astype(o_ref.dtype)

    @pl.when(j == 0)
    def _():
        matmuls(0)

    for parity in range(2):
        @pl.when((j > 0) & (j < nj) & (j % 2 == parity))
        def _():
            epilogue(1 - parity)
            matmuls(parity)

    @pl.when(j == nj)
    def _():
        epilogue((nj - 1) % 2)


def _ffn_up(x1, norm_g, mod3, w_up_bf16, conv_tab, seq_len, ffp, taps):
    N, D = x1.shape
    tm = _tile(seq_len, 1024)
    tn = _tile(ffp, 512)
    halo = BF16_SUBLANES
    assert taps - 1 <= halo and tm % halo == 0
    tpb = seq_len // tm
    nj = ffp // tn
    hb = tm // halo
    vmem = (tm * D * 4 + 2 * halo * D * 4 + (tm + halo) * D * 2 + 2 * 2 * D * tn * 2
            + 2 * tm * tn * 2 + 4 * (tm + halo) * tn * 4)

    def w_tile(j):
        return jnp.minimum(j, nj - 1)

    def e_tile(j):
        return jnp.maximum(j - 1, 0)

    return pl.pallas_call(
        functools.partial(_ffn_up_kernel, tm=tm, halo=halo, seq_len=seq_len, taps=taps, nj=nj),
        grid=(N // tm, nj + 1),
        in_specs=[
            pl.BlockSpec((tm, D), lambda i, j: (i, 0), pipeline_mode=pl.Buffered(1)),
            pl.BlockSpec((halo, D), lambda i, j: (jnp.maximum(i * hb - 1, 0), 0)),
            pl.BlockSpec((1, D), lambda i, j: (0, 0)),
            pl.BlockSpec((None, 1, D), lambda i, j: (i // tpb, 0, 4)),
            pl.BlockSpec((None, 1, D), lambda i, j: (i // tpb, 0, 3)),
            pl.BlockSpec((D, tn), lambda i, j: (0, w_tile(j))),
            pl.BlockSpec((D, tn), lambda i, j: (0, nj + w_tile(j))),
            pl.BlockSpec((8, tn), lambda i, j: (0, e_tile(j))),
            pl.BlockSpec((8, tn), lambda i, j: (0, nj + e_tile(j))),
        ],
        out_specs=pl.BlockSpec((tm, tn), lambda i, j: (i, e_tile(j))),
        out_shape=jax.ShapeDtypeStruct((N, ffp), BF16),
        scratch_shapes=[
            pltpu.VMEM((tm + halo, D), BF16),
            pltpu.VMEM((2, tm + halo, tn), F32),
            pltpu.VMEM((2, tm + halo, tn), F32),
        ],
        compiler_params=_params(vmem, 2),
        name="ffn_up",
    )(x1, x1, norm_g.reshape(1, D), mod3, mod3, w_up_bf16, w_up_bf16, conv_tab, conv_tab)


def _ffn_down_kernel(a_ref, w_ref, x_ref, g_ref, fg_ref, o_ref, *, tm, nk, final_norm):
    k = pl.program_id(1)

    @pl.when(k == 0)
    def _():
        o_ref[...] = jnp.zeros(o_ref.shape, F32)

    o_ref[...] += jnp.dot(a_ref[...], w_ref[...], preferred_element_type=F32)

    @pl.when(k == nk - 1)
    def _():
        gate = g_ref[...]
        fg = fg_ref[...]
        rows = min(NORM_ROWS, tm)

        def body(r, carry):
            sl = pl.ds(r * rows, rows)
            x2 = x_ref[sl, :] + gate * o_ref[sl, :]
            if final_norm:
                ms = jnp.mean(x2 * x2, axis=-1, keepdims=True)
                x2 = (x2 * lax.rsqrt(ms + EPS)) * fg
            o_ref[sl, :] = x2
            return carry

        lax.fori_loop(0, tm // rows, body, 0)


def _ffn_down(act, w_bf16, x1, mod3, final_g, seq_len, final_norm):
    N, D = x1.shape
    K = act.shape[1]
    tm = _tile(seq_len, 512)
    tk = _tile(K, 1024)
    tpb = seq_len // tm
    nk = K // tk
    vmem = 2 * tm * tk * 2 + 2 * tk * D * 2 + tm * D * 4 + 2 * tm * D * 4 + tm * D * 4
    return pl.pallas_call(
        functools.partial(_ffn_down_kernel, tm=tm, nk=nk, final_norm=final_norm),
        grid=(N // tm, nk),
        in_specs=[
            pl.BlockSpec((tm, tk), lambda i, k: (i, k)),
            pl.BlockSpec((tk, D), lambda i, k: (k, 0)),
            pl.BlockSpec((tm, D), lambda i, k: (i, 0), pipeline_mode=pl.Buffered(1)),
            pl.BlockSpec((None, 1, D), lambda i, k: (i // tpb, 0, 5)),
            pl.BlockSpec((1, D), lambda i, k: (0, 0)),
        ],
        out_specs=pl.BlockSpec((tm, D), lambda i, k: (i, 0)),
        out_shape=jax.ShapeDtypeStruct((N, D), F32),
        compiler_params=_params(vmem, 2),
        name="ffn_down",
    )(act, w_bf16, x1, mod3, final_g.reshape(1, D))


def _round_up(n, m):
    return (n + m - 1) // m * m


def kernel(x, c, ada_w, ada_b, norm1_g, w_in, lq1, lk1, lq2, lk2, subln_g, a_re, a_im, b_re, b_im,
           c_re, c_im, d_skip, log_dt, w_glu, w_attn_br, w_ssm_br, w_out, norm2_g, w_up, conv_w,
           conv_b, w_down, final_g):
    B, L, D = x.shape
    depth = ada_w.shape[0]
    d = lq1.shape[-1]
    aw = w_attn_br.shape[1]
    sw = w_ssm_br.shape[1]
    n_heads = aw // (2 * d)
    ff = w_down.shape[1]
    taps = conv_w.shape[1]
    ffp = _round_up(ff, 1024) if ff > 1024 else ff
    u_col0 = 3 * aw
    ga_col0 = u_col0 + sw
    gs_col0 = ga_col0 + D

    xc = x.reshape(B * L, D)
    for l in range(depth):
        lam_init = 0.8 - 0.6 * math.exp(-0.3 * l)
        lam = (jnp.exp(jnp.sum(lq1[l].astype(F32) * lk1[l].astype(F32)))
               - jnp.exp(jnp.sum(lq2[l].astype(F32) * lk2[l].astype(F32))) + lam_init)
        mod3 = _ada_mod(c, ada_w[l], ada_b[l])

        proj = _in_proj(xc, norm1_g[l], mod3, w_in[l].astype(BF16), L)
        attn = _diff_attention(proj, lam, subln_g[l], B, L, n_heads, d, lam_init)
        tables = _s5_tables(a_re[l], a_im[l], b_re[l], b_im[l], c_re[l], c_im[l], d_skip[l], log_dt[l])
        y = _s5_scan(proj, u_col0, tables, B, L)
        ssm = _glu(y, w_glu[l].astype(BF16))
        merged = _merge(attn, ssm, w_attn_br[l].astype(BF16), w_ssm_br[l].astype(BF16),
                        proj, ga_col0, gs_col0)
        x1 = _out_proj(merged, w_out[l].astype(BF16), xc, mod3, L)

        pad = ffp - ff
        w_up_p = jnp.concatenate([jnp.pad(w_up[l][:, :ff], ((0, 0), (0, pad))),
                                  jnp.pad(w_up[l][:, ff:], ((0, 0), (0, pad)))], axis=1).astype(BF16)
        conv_rows = jnp.concatenate([conv_w[l], conv_b[l][None]], axis=0).astype(F32)
        conv_tab = jnp.concatenate([jnp.pad(conv_rows[:, :ff], ((0, 8 - taps - 1), (0, pad))),
                                    jnp.pad(conv_rows[:, ff:], ((0, 8 - taps - 1), (0, pad)))], axis=1)
        w_down_p = jnp.pad(w_down[l], ((0, pad), (0, 0))).astype(BF16)
        act = _ffn_up(x1, norm2_g[l], mod3, w_up_p, conv_tab, L, ffp, taps)
        xc = _ffn_down(act, w_down_p, x1, mod3, final_g, L, final_norm=(l == depth - 1))
    return xc.reshape(B, L, D)
```

```python
import functools
import math

import jax
import jax.numpy as jnp
from jax import lax
from jax.experimental import pallas as pl
from jax.experimental.pallas import tpu as pltpu

F32 = jnp.float32
BF16 = jnp.bfloat16
EPS = 1e-6
NEG_BIG = -1e30
LOG2E = 1.0 / math.log(2.0)

V7X_VMEM_BYTES = 64 * 1024 * 1024
V7X_LANES = 128
V7X_MXU_DIM = 256
BF16_SUBLANES = 16
S5_CHUNK = 16
NORM_ROWS = 32
ATTN_BLOCK = 512


def _vmem_limit(block_bytes):
    return int(min(block_bytes + 12 * 1024 * 1024, V7X_VMEM_BYTES - 1024 * 1024))


def _tile(dim, want):
    t = min(dim, want)
    assert dim % t == 0, (dim, want)
    return t


def _params(vmem_bytes, n_grid):
    return pltpu.CompilerParams(
        dimension_semantics=("arbitrary",) * n_grid,
        vmem_limit_bytes=_vmem_limit(vmem_bytes),
    )


def _ada_kernel(c_ref, w_ref, b_ref, o_ref):
    w = w_ref[...].astype(BF16)
    o_ref[...] = jnp.dot(c_ref[...], w, preferred_element_type=F32) + b_ref[...]


def _ada_mod(c, ada_w, ada_b):
    B, D = c.shape
    n_out = ada_w.shape[1]
    rows = 8
    c_pad = jnp.zeros((rows, D), BF16).at[:B].set(c.astype(BF16))
    tn = _tile(n_out, 512)
    out = pl.pallas_call(
        _ada_kernel,
        grid=(n_out // tn,),
        in_specs=[
            pl.BlockSpec((rows, D), lambda j: (0, 0)),
            pl.BlockSpec((D, tn), lambda j: (0, j)),
            pl.BlockSpec((1, tn), lambda j: (0, j)),
        ],
        out_specs=pl.BlockSpec((rows, tn), lambda j: (0, j)),
        out_shape=jax.ShapeDtypeStruct((rows, n_out), F32),
        compiler_params=_params(2 * D * tn * 4 + D * tn * 2, 1),
        name="ada_mod",
    )(c_pad, ada_w, ada_b.reshape(1, n_out))
    return out[:B].reshape(B, 1, n_out)


def _adaln_rows(x_ref, g_ref, sc_ref, sh_ref, h_ref, n_rows, dst_row0):
    g = g_ref[...]
    one_sc = 1.0 + sc_ref[...]
    sh = sh_ref[...]
    rows = min(NORM_ROWS, n_rows)

    def body(r, carry):
        x = x_ref[pl.ds(r * rows, rows), :]
        ms = jnp.mean(x * x, axis=-1, keepdims=True)
        y = (x * lax.rsqrt(ms + EPS)) * g
        h_ref[pl.ds(dst_row0 + r * rows, rows), :] = (y * one_sc + sh).astype(BF16)
        return carry

    lax.fori_loop(0, n_rows // rows, body, 0)


def _in_proj_kernel(x_ref, g_ref, sc_ref, sh_ref, w_ref, o_ref, h_ref, *, tm):
    @pl.when(pl.program_id(1) == 0)
    def _():
        _adaln_rows(x_ref, g_ref, sc_ref, sh_ref, h_ref, tm, 0)

    o_ref[...] = jnp.dot(h_ref[...], w_ref[...], preferred_element_type=F32).astype(o_ref.dtype)


def _in_proj(x2d, norm_g, mod3, w_bf16, seq_len):
    N, D = x2d.shape
    n_out = w_bf16.shape[1]
    tm = _tile(seq_len, 1024)
    tn = _tile(n_out, 512)
    tpb = seq_len // tm
    vmem = 2 * tm * D * 4 + tm * D * 2 + 2 * D * tn * 2 + 2 * tm * tn * 2 + tm * tn * 4
    return pl.pallas_call(
        functools.partial(_in_proj_kernel, tm=tm),
        grid=(N // tm, n_out // tn),
        in_specs=[
            pl.BlockSpec((tm, D), lambda i, j: (i, 0)),
            pl.BlockSpec((1, D), lambda i, j: (0, 0)),
            pl.BlockSpec((None, 1, D), lambda i, j: (i // tpb, 0, 1)),
            pl.BlockSpec((None, 1, D), lambda i, j: (i // tpb, 0, 0)),
            pl.BlockSpec((D, tn), lambda i, j: (0, j)),
        ],
        out_specs=pl.BlockSpec((tm, tn), lambda i, j: (i, j)),
        out_shape=jax.ShapeDtypeStruct((N, n_out), BF16),
        scratch_shapes=[pltpu.VMEM((tm, D), BF16)],
        compiler_params=_params(vmem, 2),
        name="in_proj",
    )(x2d, norm_g.reshape(1, D), mod3, mod3, w_bf16)


def _attn_kernel(slope_ref, lam_ref, q_ref, k_ref, v_ref, g_ref, o_ref,
                 qa_ref, ek_ref, vt_ref, s_ref, m_ref, l_ref, acc_ref, *, tq, d, out_scale):
    h = pl.program_id(1)
    qi = pl.program_id(2)
    lam = lam_ref[0]
    tk = tq
    hw = 2 * d
    nkb = vt_ref.shape[0]
    slope2 = slope_ref[h] * LOG2E

    @pl.when(qi == 0)
    def _():
        for c in range(nkb):
            vt_ref[c] = v_ref[c * tk:(c + 1) * tk, :].astype(F32).T.astype(BF16)
        jj = lax.broadcasted_iota(jnp.int32, (tk, hw), 0)
        ln = lax.broadcasted_iota(jnp.int32, (tk, hw), 1)
        hi = ((jj // 32) * 32).astype(F32)
        lo = (jj % 32).astype(F32)
        ek_ref[...] = jnp.where(ln < 3, hi, jnp.where(ln < 6, lo, 0.0)).astype(BF16)

    qf = q_ref[...].astype(F32) * (d ** -0.5 * LOG2E)
    lane = lax.broadcasted_iota(jnp.int32, (tq, hw), 1)
    qa_ref[0:tq, 0:hw] = jnp.where(lane < d, qf, 0.0).astype(BF16)
    qa_ref[tq:2 * tq, 0:hw] = jnp.where(lane >= d, qf, 0.0).astype(BF16)
    sl = jnp.full((1, hw), slope2, F32)
    pa = sl.astype(BF16).astype(F32)
    ra = sl - pa
    pb = ra.astype(BF16).astype(F32)
    pc = (ra - pb).astype(BF16).astype(F32)
    l1 = lax.broadcasted_iota(jnp.int32, (1, hw), 1)
    piece = jnp.where(l1 % 3 == 0, pa, jnp.where(l1 % 3 == 1, pb, pc))
    eq = jnp.where(l1 < 6, piece, 0.0)
    qa_ref[:, hw:2 * hw] = jnp.broadcast_to(eq, (2 * tq, hw)).astype(BF16)

    m_ref[...] = jnp.full(m_ref.shape, NEG_BIG, F32)
    l_ref[...] = jnp.zeros(l_ref.shape, F32)
    acc_ref[...] = jnp.zeros(acc_ref.shape, F32)

    def scores(kb, slot):
        k = k_ref[pl.ds(pl.multiple_of(kb * tk, tk), tk), :]
        ka = jnp.concatenate([k, ek_ref[...]], axis=1)
        s_ref[slot] = lax.dot_general(ka, qa_ref[...], (((1,), (1,)), ((), ())),
                                      preferred_element_type=F32)

    def softmax_pv(kb, slot, masked):
        s = s_ref[slot]
        if masked:
            key = lax.broadcasted_iota(jnp.int32, (tk, 2 * tq), 0)
            qry = lax.broadcasted_iota(jnp.int32, (tk, 2 * tq), 1)
            qry = jnp.where(qry >= tq, qry - tq, qry)
            s = jnp.where(key <= qry, s, NEG_BIG)
        off = slope2 * ((kb - qi) * tk).astype(F32)
        m_old = m_ref[...]
        m_new = jnp.maximum(m_old, jnp.max(s, axis=0, keepdims=True) + off)
        alpha = jnp.exp2(m_old - m_new)
        p = jnp.exp2(s - (m_new - off))
        l_ref[...] = alpha * l_ref[...] + jnp.sum(p, axis=0, keepdims=True)
        pv = jnp.dot(vt_ref[kb], p.astype(BF16), preferred_element_type=F32)
        acc_ref[...] = alpha * acc_ref[...] + pv
        m_ref[...] = m_new

    scores(0, 0)

    def pair(i, carry):
        kb = 2 * i
        scores(kb + 1, 1)
        softmax_pv(kb, 0, False)
        scores(kb + 2, 0)
        softmax_pv(kb + 1, 1, False)
        return carry

    lax.fori_loop(0, qi // 2, pair, 0)

    @pl.when(qi % 2 == 0)
    def _():
        softmax_pv(qi, 0, True)

    @pl.when(qi % 2 == 1)
    def _():
        scores(qi, 1)
        softmax_pv(qi - 1, 0, False)
        softmax_pv(qi, 1, True)

    inv_l = 1.0 / l_ref[...]
    o = acc_ref[:, 0:tq] * inv_l[:, 0:tq] - lam * (acc_ref[:, tq:2 * tq] * inv_l[:, tq:2 * tq])
    ms = jnp.mean(o * o, axis=0, keepdims=True)
    y = (o * lax.rsqrt(ms + EPS)).T
    o_ref[...] = (y * g_ref[...] * out_scale).astype(o_ref.dtype)


def _diff_attention(proj, lam, subln_g, batch, seq_len, n_heads, d, lam_init):
    N = proj.shape[0]
    hw = 2 * d
    assert hw == V7X_LANES
    tq = _tile(seq_len, ATTN_BLOCK)
    assert tq % 32 == 0 and tq <= 32 * 256
    nq = seq_len // tq
    slopes = jnp.exp2(-8.0 * jnp.arange(1, n_heads + 1, dtype=F32) / n_heads)
    vmem = (2 * 2 * tq * hw * 2 + 2 * 2 * seq_len * hw * 2 + seq_len * hw * 2
            + 2 * tq * 2 * hw * 2 + tq * hw * 2 + 2 * tq * 2 * tq * 4
            + hw * 2 * tq * 4 + 3 * tq * 2 * tq * 4)
    grid_spec = pltpu.PrefetchScalarGridSpec(
        num_scalar_prefetch=2,
        grid=(batch, n_heads, nq),
        in_specs=[
            pl.BlockSpec((tq, hw), lambda b, h, qi, *_: (b * nq + qi, h)),
            pl.BlockSpec((seq_len, hw), lambda b, h, qi, *_: (b, n_heads + h)),
            pl.BlockSpec((seq_len, hw), lambda b, h, qi, *_: (b, 2 * n_heads + h)),
            pl.BlockSpec((1, hw), lambda b, h, qi, *_: (0, 0)),
        ],
        out_specs=pl.BlockSpec((tq, hw), lambda b, h, qi, *_: (b * nq + qi, h)),
        scratch_shapes=[
            pltpu.VMEM((2 * tq, 2 * hw), BF16),
            pltpu.VMEM((tq, hw), BF16),
            pltpu.VMEM((seq_len // tq, hw, tq), BF16),
            pltpu.VMEM((2, tq, 2 * tq), F32),
            pltpu.VMEM((1, 2 * tq), F32),
            pltpu.VMEM((1, 2 * tq), F32),
            pltpu.VMEM((hw, 2 * tq), F32),
        ],
    )
    return pl.pallas_call(
        functools.partial(_attn_kernel, tq=tq, d=d, out_scale=1.0 - lam_init),
        grid_spec=grid_spec,
        out_shape=jax.ShapeDtypeStruct((N, n_heads * hw), BF16),
        compiler_params=_params(vmem, 3),
        name="diff_attn",
    )(slopes, lam.reshape(1), proj, proj, proj, subln_g.reshape(1, hw))


def _s5_tables(a_re, a_im, b_re, b_im, c_re, c_im, d_skip, log_dt):
    hp = lax.Precision.HIGHEST
    G, P = a_re.shape
    cch = b_re.shape[-1]
    T = S5_CHUNK
    lam = lax.complex(a_re.astype(F32), a_im.astype(F32))
    dt = jnp.exp(log_dt.astype(F32))[:, None]
    lam_dt = lam * dt
    a_bar = jnp.exp(lam_dt)
    b_bar = ((a_bar - 1.0) / lam)[..., None] * lax.complex(b_re.astype(F32), b_im.astype(F32))
    c_mat = lax.complex(c_re.astype(F32), c_im.astype(F32))
    pw = jnp.exp(lam_dt[None] * jnp.arange(0, T + 1, dtype=F32)[:, None, None])

    w_c = jnp.einsum('sgp,gpc->gscp', pw[T - 1 - jnp.arange(T)], b_bar).reshape(G, T * cch, P)
    w_tab = jnp.concatenate([w_c.real, w_c.imag, w_c.imag, w_c.real], axis=-1)

    m_c = jnp.einsum('gcp,tgp->gptc', c_mat, pw[1:T + 1]).reshape(G, P, T * cch)
    cp_tab = jnp.concatenate([m_c.real, -m_c.imag], axis=1)

    cr, ci = c_mat.real, c_mat.imag
    ab = pw[:T][:, :, :, None] * b_bar[None]
    k_tab = (jnp.einsum('gcp,kgpd->kgcd', cr, ab.real, precision=hp)
             - jnp.einsum('gcp,kgpd->kgcd', ci, ab.imag, precision=hp))
    k_tab = k_tab.at[0].add(jnp.eye(cch, dtype=F32)[None] * d_skip.astype(F32).reshape(G, cch, 1))
    lag = jnp.arange(T)[None, :] - jnp.arange(T)[:, None]
    tz = k_tab[jnp.clip(lag, 0, T - 1)]
    tz = jnp.where((lag >= 0)[:, :, None, None, None], tz, 0.0)
    tz_tab = tz.transpose(2, 0, 4, 1, 3).reshape(G, T * cch, T * cch)

    at = pw[T]
    a_tab = jnp.stack([
        jnp.concatenate([at.real, at.real], axis=-1),
        jnp.concatenate([-at.imag, at.imag], axis=-1),
        jnp.concatenate([at.imag, -at.imag], axis=-1),
    ], axis=1)
    return w_tab.astype(BF16), cp_tab.astype(BF16), tz_tab.astype(BF16), a_tab.astype(F32)


def _s5_kernel(u_ref, w_ref, cp_ref, tz_ref, a_ref, y_ref, xf_ref, ug_ref, c_ref, s_ref,
               *, gb, cch, n_batch, nch, rb):
    T = S5_CHUNK
    p2 = s_ref.shape[-1]
    lanes = gb * cch
    halves = T // gb
    rows = n_batch * nch
    gran = lax.broadcasted_iota(jnp.int32, (1, lanes), 1) // cch

    def gather_rows(t, carry):
        tok0 = pl.multiple_of(t * (rb * T), rb * T)
        r0 = pl.multiple_of(t * rb, rb)
        xf_ref[...] = u_ref[pl.ds(tok0, rb * T), :].astype(F32)
        x = [xf_ref[pl.ds(s, rb, stride=T), :] for s in range(T)]
        out = [[None] * halves for _ in range(gb)]
        for hf in range(halves):
            for k in range(gb):
                z = jnp.zeros((rb, lanes), F32)
                for s8 in range(gb):
                    z = jnp.where(gran == (s8 - k) % gb, x[hf * gb + s8], z)
                if k:
                    z = pltpu.roll(z, k * cch, axis=1)
                for g in range(gb):
                    prev = out[g][hf] if out[g][hf] is not None else jnp.zeros((rb, lanes), F32)
                    out[g][hf] = jnp.where(gran == (g + k) % gb, z, prev)
        for g in range(gb):
            ug_ref[g, pl.ds(r0, rb), :] = jnp.concatenate(out[g], axis=1).astype(BF16)
        return carry

    lax.fori_loop(0, rows // rb, gather_rows, 0)

    for g in range(gb):
        c_ref[g] = jnp.dot(ug_ref[g], w_ref[g], preferred_element_type=F32)

    a1 = [a_ref[g, 0:1, :] for g in range(gb)]
    a2 = [a_ref[g, 1:2, :] for g in range(gb)]
    a3 = [a_ref[g, 2:3, :] for g in range(gb)]
    chains = [(g, b) for g in range(gb) for b in range(n_batch)]

    def step(j, carry):
        out = []
        for (g, b), (x, xs) in zip(chains, carry):
            r = b * nch + j
            s_ref[g, pl.ds(r, 1), :] = x
            cc = c_ref[g, pl.ds(r, 1), :]
            out.append((a1[g] * x + a2[g] * xs + cc[:, :p2],
                        a1[g] * xs + a3[g] * x + cc[:, p2:]))
        return tuple(out)

    zero = jnp.zeros((1, p2), F32)
    lax.fori_loop(0, nch, step, tuple((zero, zero) for _ in chains))

    for g in range(gb):
        y = jnp.dot(s_ref[g].astype(BF16), cp_ref[g], preferred_element_type=F32)
        c_ref[g] = y + jnp.dot(ug_ref[g], tz_ref[g], preferred_element_type=F32)

    def scatter_rows(t, carry):
        tok0 = pl.multiple_of(t * (rb * T), rb * T)
        r0 = pl.multiple_of(t * rb, rb)
        yg = [[c_ref[g, pl.ds(r0, rb), hf * lanes:(hf + 1) * lanes] for hf in range(halves)]
              for g in range(gb)]
        y = [None] * T
        for hf in range(halves):
            for k in range(gb):
                z = jnp.zeros((rb, lanes), F32)
                for g in range(gb):
                    z = jnp.where(gran == (g + k) % gb, yg[g][hf], z)
                if k:
                    z = pltpu.roll(z, (gb - k) * cch, axis=1)
                for s8 in range(gb):
                    s = hf * gb + s8
                    prev = y[s] if y[s] is not None else jnp.zeros((rb, lanes), F32)
                    y[s] = jnp.where(gran == (s8 - k) % gb, z, prev)
        for s in range(T):
            xf_ref[pl.ds(s, rb, stride=T), :] = y[s]
        y_ref[pl.ds(tok0, rb * T), :] = xf_ref[...].astype(y_ref.dtype)
        return carry

    lax.fori_loop(0, rows // rb, scatter_rows, 0)


def _s5_scan(proj, u_col0, tables, batch, seq_len):
    w_tab, cp_tab, tz_tab, a_tab = tables
    N = proj.shape[0]
    G = w_tab.shape[0]
    tc = w_tab.shape[1]
    p2 = cp_tab.shape[1]
    cch = tc // S5_CHUNK
    nch = seq_len // S5_CHUNK
    rows = batch * nch
    gb = V7X_LANES // cch
    assert G % gb == 0 and S5_CHUNK % gb == 0 and u_col0 % V7X_LANES == 0
    assert 2 * p2 == tc
    rb = _tile(rows, 128)
    u_blk0 = u_col0 // V7X_LANES
    vmem = (2 * 2 * N * V7X_LANES * 2 + 2 * gb * (tc * 2 * p2 + p2 * tc + tc * tc) * 2
            + rb * S5_CHUNK * V7X_LANES * 4 + gb * rows * tc * 2
            + gb * rows * 2 * p2 * 4 + gb * rows * p2 * 4 + 2 * rows * tc * 4)
    return pl.pallas_call(
        functools.partial(_s5_kernel, gb=gb, cch=cch, n_batch=batch, nch=nch, rb=rb),
        grid=(G // gb,),
        in_specs=[
            pl.BlockSpec((N, V7X_LANES), lambda i: (0, u_blk0 + i)),
            pl.BlockSpec((gb, tc, 2 * p2), lambda i: (i, 0, 0)),
            pl.BlockSpec((gb, p2, tc), lambda i: (i, 0, 0)),
            pl.BlockSpec((gb, tc, tc), lambda i: (i, 0, 0)),
            pl.BlockSpec((gb, 3, p2), lambda i: (i, 0, 0)),
        ],
        out_specs=pl.BlockSpec((N, V7X_LANES), lambda i: (0, i)),
        out_shape=jax.ShapeDtypeStruct((N, G * cch), BF16),
        scratch_shapes=[
            pltpu.VMEM((rb * S5_CHUNK, V7X_LANES), F32),
            pltpu.VMEM((gb, rows, tc), BF16),
            pltpu.VMEM((gb, rows, 2 * p2), F32),
            pltpu.VMEM((gb, rows, p2), F32),
        ],
        compiler_params=_params(vmem, 1),
        name="s5_scan",
    )(proj, w_tab, cp_tab, tz_tab, a_tab)


def _gelu_tanh(x):
    return 0.5 * x * (1.0 + jnp.tanh(math.sqrt(2.0 / math.pi) * (x + 0.044715 * (x * x * x))))


def _glu_kernel(y_ref, w_ref, o_ref):
    z = _gelu_tanh(y_ref[...].astype(F32))
    gate = jnp.dot(z.astype(BF16), w_ref[...], preferred_element_type=F32)
    o_ref[...] = (z * jax.nn.sigmoid(gate)).astype(o_ref.dtype)


def _glu(y, w_bf16):
    N, W = y.shape
    tm = _tile(N, 512)
    vmem = 2 * tm * W * 2 + 2 * W * W * 2 + 2 * tm * W * 2 + 4 * tm * W * 4
    return pl.pallas_call(
        _glu_kernel,
        grid=(N // tm,),
        in_specs=[pl.BlockSpec((tm, W), lambda i: (i, 0)),
                  pl.BlockSpec((W, W), lambda i: (0, 0))],
        out_specs=pl.BlockSpec((tm, W), lambda i: (i, 0)),
        out_shape=jax.ShapeDtypeStruct((N, W), BF16),
        compiler_params=_params(vmem, 1),
        name="glu",
    )(y, w_bf16)


def _merge_kernel(a_ref, s_ref, wa_ref, ws_ref, ga_ref, gs_ref, o_ref):
    ya = jnp.dot(a_ref[...], wa_ref[...], preferred_element_type=F32)
    ys = jnp.dot(s_ref[...], ws_ref[...], preferred_element_type=F32)
    o = jax.nn.sigmoid(ga_ref[...].astype(F32)) * ya + jax.nn.sigmoid(gs_ref[...].astype(F32)) * ys
    o_ref[...] = o.astype(o_ref.dtype)


def _merge(attn, ssm, wa_bf16, ws_bf16, proj, ga_col0, gs_col0):
    N, AW = attn.shape
    SW = ssm.shape[1]
    D = wa_bf16.shape[1]
    tm = _tile(N, 1024)
    tn = _tile(D, 512)
    assert ga_col0 % tn == 0 and gs_col0 % tn == 0
    ga_blk, gs_blk = ga_col0 // tn, gs_col0 // tn
    vmem = (2 * tm * (AW + SW) * 2 + 2 * (AW + SW) * tn * 2 + 3 * 2 * tm * tn * 2 + 3 * tm * tn * 4)
    return pl.pallas_call(
        _merge_kernel,
        grid=(N // tm, D // tn),
        in_specs=[
            pl.BlockSpec((tm, AW), lambda i, j: (i, 0)),
            pl.BlockSpec((tm, SW), lambda i, j: (i, 0)),
            pl.BlockSpec((AW, tn), lambda i, j: (0, j)),
            pl.BlockSpec((SW, tn), lambda i, j: (0, j)),
            pl.BlockSpec((tm, tn), lambda i, j: (i, ga_blk + j)),
            pl.BlockSpec((tm, tn), lambda i, j: (i, gs_blk + j)),
        ],
        out_specs=pl.BlockSpec((tm, tn), lambda i, j: (i, j)),
        out_shape=jax.ShapeDtypeStruct((N, D), BF16),
        compiler_params=_params(vmem, 2),
        name="merge",
    )(attn, ssm, wa_bf16, ws_bf16, proj, proj)


def _out_proj_kernel(m_ref, w_ref, x_ref, g_ref, o_ref):
    o_ref[...] = x_ref[...] + g_ref[...] * jnp.dot(m_ref[...], w_ref[...], preferred_element_type=F32)


def _out_proj(merged, w_bf16, x2d, mod3, seq_len):
    N, D = x2d.shape
    K = merged.shape[1]
    tm = _tile(seq_len, 1024)
    tn = _tile(D, 512)
    tpb = seq_len // tm
    g1_blk = 2 * (D // tn)
    vmem = 2 * tm * K * 2 + 2 * K * tn * 2 + 4 * tm * tn * 4 + tm * tn * 4
    return pl.pallas_call(
        _out_proj_kernel,
        grid=(N // tm, D // tn),
        in_specs=[
            pl.BlockSpec((tm, K), lambda i, j: (i, 0)),
            pl.BlockSpec((K, tn), lambda i, j: (0, j)),
            pl.BlockSpec((tm, tn), lambda i, j: (i, j)),
            pl.BlockSpec((None, 1, tn), lambda i, j: (i // tpb, 0, g1_blk + j)),
        ],
        out_specs=pl.BlockSpec((tm, tn), lambda i, j: (i, j)),
        out_shape=jax.ShapeDtypeStruct((N, D), F32),
        compiler_params=_params(vmem, 2),
        name="out_proj",
    )(merged, w_bf16, x2d, mod3)


def _ffn_up_kernel(x_ref, halo_ref, g_ref, sc_ref, sh_ref, wa_ref, wg_ref, cwa_ref, cwg_ref,
                   o_ref, h_ref, ua_ref, ug_ref, *, tm, halo, seq_len, taps):
    i = pl.program_id(0)

    @pl.when(pl.program_id(1) == 0)
    def _():
        _adaln_rows(x_ref, g_ref, sc_ref, sh_ref, h_ref, tm, halo)
        _adaln_rows(halo_ref, g_ref, sc_ref, sh_ref, h_ref, halo, 0)

        @pl.when((i * tm) % seq_len == 0)
        def _():
            h_ref[0:halo, :] = jnp.zeros((halo, h_ref.shape[1]), BF16)

    h = h_ref[...]
    ua_ref[...] = jnp.dot(h, wa_ref[...], preferred_element_type=F32)
    ug_ref[...] = jnp.dot(h, wg_ref[...], preferred_element_type=F32)

    def conv(u_ref, cw_ref):
        out = cw_ref[taps:taps + 1, :]
        for j in range(taps):
            out = out + cw_ref[j:j + 1, :] * u_ref[pl.ds(halo - (taps - 1) + j, tm), :]
        return out

    a = conv(ua_ref, cwa_ref)
    g = conv(ug_ref, cwg_ref)
    o_ref[...] = (g * jax.nn.sigmoid(g) * a).astype(o_ref.dtype)


def _ffn_up(x1, norm_g, mod3, w_up_bf16, conv_tab, seq_len, ffp, taps):
    N, D = x1.shape
    tm = _tile(seq_len, 1024)
    tn = _tile(ffp, 512)
    halo = BF16_SUBLANES
    assert taps - 1 <= halo and tm % halo == 0
    tpb = seq_len // tm
    nj = ffp // tn
    hb = tm // halo
    vmem = (tm * D * 4 + 2 * halo * D * 4 + (tm + halo) * D * 2 + 2 * 2 * D * tn * 2
            + 2 * tm * tn * 2 + 2 * (tm + halo) * tn * 4 + 4 * tm * tn * 4)
    return pl.pallas_call(
        functools.partial(_ffn_up_kernel, tm=tm, halo=halo, seq_len=seq_len, taps=taps),
        grid=(N // tm, nj),
        in_specs=[
            pl.BlockSpec((tm, D), lambda i, j: (i, 0), pipeline_mode=pl.Buffered(1)),
            pl.BlockSpec((halo, D), lambda i, j: (jnp.maximum(i * hb - 1, 0), 0)),
            pl.BlockSpec((1, D), lambda i, j: (0, 0)),
            pl.BlockSpec((None, 1, D), lambda i, j: (i // tpb, 0, 4)),
            pl.BlockSpec((None, 1, D), lambda i, j: (i // tpb, 0, 3)),
            pl.BlockSpec((D, tn), lambda i, j: (0, j)),
            pl.BlockSpec((D, tn), lambda i, j: (0, nj + j)),
            pl.BlockSpec((8, tn), lambda i, j: (0, j)),
            pl.BlockSpec((8, tn), lambda i, j: (0, nj + j)),
        ],
        out_specs=pl.BlockSpec((tm, tn), lambda i, j: (i, j)),
        out_shape=jax.ShapeDtypeStruct((N, ffp), BF16),
        scratch_shapes=[
            pltpu.VMEM((tm + halo, D), BF16),
            pltpu.VMEM((tm + halo, tn), F32),
            pltpu.VMEM((tm + halo, tn), F32),
        ],
        compiler_params=_params(vmem, 2),
        name="ffn_up",
    )(x1, x1, norm_g.reshape(1, D), mod3, mod3, w_up_bf16, w_up_bf16, conv_tab, conv_tab)


def _ffn_down_kernel(a_ref, w_ref, x_ref, g_ref, fg_ref, o_ref, *, tm, nk, final_norm):
    k = pl.program_id(1)

    @pl.when(k == 0)
    def _():
        o_ref[...] = jnp.zeros(o_ref.shape, F32)

    o_ref[...] += jnp.dot(a_ref[...], w_ref[...], preferred_element_type=F32)

    @pl.when(k == nk - 1)
    def _():
        gate = g_ref[...]
        fg = fg_ref[...]
        rows = min(NORM_ROWS, tm)

        def body(r, carry):
            sl = pl.ds(r * rows, rows)
            x2 = x_ref[sl, :] + gate * o_ref[sl, :]
            if final_norm:
                ms = jnp.mean(x2 * x2, axis=-1, keepdims=True)
                x2 = (x2 * lax.rsqrt(ms + EPS)) * fg
            o_ref[sl, :] = x2
            return carry

        lax.fori_loop(0, tm // rows, body, 0)


def _ffn_down(act, w_bf16, x1, mod3, final_g, seq_len, final_norm):
    N, D = x1.shape
    K = act.shape[1]
    tm = _tile(seq_len, 512)
    tk = _tile(K, 1024)
    tpb = seq_len // tm
    nk = K // tk
    vmem = 2 * tm * tk * 2 + 2 * tk * D * 2 + tm * D * 4 + 2 * tm * D * 4 + tm * D * 4
    return pl.pallas_call(
        functools.partial(_ffn_down_kernel, tm=tm, nk=nk, final_norm=final_norm),
        grid=(N // tm, nk),
        in_specs=[
            pl.BlockSpec((tm, tk), lambda i, k: (i, k)),
            pl.BlockSpec((tk, D), lambda i, k: (k, 0)),
            pl.BlockSpec((tm, D), lambda i, k: (i, 0), pipeline_mode=pl.Buffered(1)),
            pl.BlockSpec((None, 1, D), lambda i, k: (i // tpb, 0, 5)),
            pl.BlockSpec((1, D), lambda i, k: (0, 0)),
        ],
        out_specs=pl.BlockSpec((tm, D), lambda i, k: (i, 0)),
        out_shape=jax.ShapeDtypeStruct((N, D), F32),
        compiler_params=_params(vmem, 2),
        name="ffn_down",
    )(act, w_bf16, x1, mod3, final_g.reshape(1, D))


def _round_up(n, m):
    return (n + m - 1) // m * m


def kernel(x, c, ada_w, ada_b, norm1_g, w_in, lq1, lk1, lq2, lk2, subln_g, a_re, a_im, b_re, b_im,
           c_re, c_im, d_skip, log_dt, w_glu, w_attn_br, w_ssm_br, w_out, norm2_g, w_up, conv_w,
           conv_b, w_down, final_g):
    B, L, D = x.shape
    depth = ada_w.shape[0]
    d = lq1.shape[-1]
    aw = w_attn_br.shape[1]
    sw = w_ssm_br.shape[1]
    n_heads = aw // (2 * d)
    ff = w_down.shape[1]
    taps = conv_w.shape[1]
    ffp = _round_up(ff, 1024) if ff > 1024 else ff
    u_col0 = 3 * aw
    ga_col0 = u_col0 + sw
    gs_col0 = ga_col0 + D

    xc = x.reshape(B * L, D)
    for l in range(depth):
        lam_init = 0.8 - 0.6 * math.exp(-0.3 * l)
        lam = (jnp.exp(jnp.sum(lq1[l].astype(F32) * lk1[l].astype(F32)))
               - jnp.exp(jnp.sum(lq2[l].astype(F32) * lk2[l].astype(F32))) + lam_init)
        mod3 = _ada_mod(c, ada_w[l], ada_b[l])

        proj = _in_proj(xc, norm1_g[l], mod3, w_in[l].astype(BF16), L)
        attn = _diff_attention(proj, lam, subln_g[l], B, L, n_heads, d, lam_init)
        tables = _s5_tables(a_re[l], a_im[l], b_re[l], b_im[l], c_re[l], c_im[l], d_skip[l], log_dt[l])
        y = _s5_scan(proj, u_col0, tables, B, L)
        ssm = _glu(y, w_glu[l].astype(BF16))
        merged = _merge(attn, ssm, w_attn_br[l].astype(BF16), w_ssm_br[l].astype(BF16),
                        proj, ga_col0, gs_col0)
        x1 = _out_proj(merged, w_out[l].astype(BF16), xc, mod3, L)

        pad = ffp - ff
        w_up_p = jnp.concatenate([jnp.pad(w_up[l][:, :ff], ((0, 0), (0, pad))),
                                  jnp.pad(w_up[l][:, ff:], ((0, 0), (0, pad)))], axis=1).astype(BF16)
        conv_rows = jnp.concatenate([conv_w[l], conv_b[l][None]], axis=0).astype(F32)
        conv_tab = jnp.concatenate([jnp.pad(conv_rows[:, :ff], ((0, 8 - taps - 1), (0, pad))),
                                    jnp.pad(conv_rows[:, ff:], ((0, 8 - taps - 1), (0, pad)))], axis=1)
        w_down_p = jnp.pad(w_down[l], ((0, pad), (0, 0))).astype(BF16)
        act = _ffn_up(x1, norm2_g[l], mod3, w_up_p, conv_tab, L, ffp, taps)
        xc = _ffn_down(act, w_down_p, x1, mod3, final_g, L, final_norm=(l == depth - 1))
    return xc.reshape(B, L, D)
```

```python
import functools
import math

import jax
import jax.numpy as jnp
from jax import lax
from jax.experimental import pallas as pl
from jax.experimental.pallas import tpu as pltpu

F32 = jnp.float32
BF16 = jnp.bfloat16
EPS = 1e-6
NEG_BIG = -1e30
LOG2E = 1.0 / math.log(2.0)

V7X_VMEM_BYTES = 64 * 1024 * 1024
V7X_LANES = 128
V7X_MXU_DIM = 256
BF16_SUBLANES = 16
S5_CHUNK = 16
NORM_ROWS = 32
ATTN_BLOCK = 512


def _vmem_limit(block_bytes):
    return int(min(block_bytes + 12 * 1024 * 1024, V7X_VMEM_BYTES - 1024 * 1024))


def _tile(dim, want):
    t = min(dim, want)
    assert dim % t == 0, (dim, want)
    return t


def _params(vmem_bytes, n_grid):
    return pltpu.CompilerParams(
        dimension_semantics=("arbitrary",) * n_grid,
        vmem_limit_bytes=_vmem_limit(vmem_bytes),
    )


def _ada_kernel(c_ref, w_ref, b_ref, o_ref):
    w = w_ref[...].astype(BF16)
    o_ref[...] = jnp.dot(c_ref[...], w, preferred_element_type=F32) + b_ref[...]


def _ada_mod(c, ada_w, ada_b):
    B, D = c.shape
    n_out = ada_w.shape[1]
    rows = 8
    c_pad = jnp.zeros((rows, D), BF16).at[:B].set(c.astype(BF16))
    tn = _tile(n_out, 512)
    out = pl.pallas_call(
        _ada_kernel,
        grid=(n_out // tn,),
        in_specs=[
            pl.BlockSpec((rows, D), lambda j: (0, 0)),
            pl.BlockSpec((D, tn), lambda j: (0, j)),
            pl.BlockSpec((1, tn), lambda j: (0, j)),
        ],
        out_specs=pl.BlockSpec((rows, tn), lambda j: (0, j)),
        out_shape=jax.ShapeDtypeStruct((rows, n_out), F32),
        compiler_params=_params(2 * D * tn * 4 + D * tn * 2, 1),
        name="ada_mod",
    )(c_pad, ada_w, ada_b.reshape(1, n_out))
    return out[:B].reshape(B, 1, n_out)


def _adaln_rows(x_ref, g_ref, sc_ref, sh_ref, h_ref, n_rows, dst_row0):
    g = g_ref[...]
    one_sc = 1.0 + sc_ref[...]
    sh = sh_ref[...]
    rows = min(NORM_ROWS, n_rows)

    def body(r, carry):
        x = x_ref[pl.ds(r * rows, rows), :]
        ms = jnp.mean(x * x, axis=-1, keepdims=True)
        y = (x * lax.rsqrt(ms + EPS)) * g
        h_ref[pl.ds(dst_row0 + r * rows, rows), :] = (y * one_sc + sh).astype(BF16)
        return carry

    lax.fori_loop(0, n_rows // rows, body, 0)


def _in_proj_kernel(x_ref, g_ref, sc_ref, sh_ref, w_ref, o_ref, h_ref, *, tm):
    @pl.when(pl.program_id(1) == 0)
    def _():
        _adaln_rows(x_ref, g_ref, sc_ref, sh_ref, h_ref, tm, 0)

    o_ref[...] = jnp.dot(h_ref[...], w_ref[...], preferred_element_type=F32).astype(o_ref.dtype)


def _in_proj(x2d, norm_g, mod3, w_bf16, seq_len):
    N, D = x2d.shape
    n_out = w_bf16.shape[1]
    tm = _tile(seq_len, 1024)
    tn = _tile(n_out, 512)
    tpb = seq_len // tm
    vmem = 2 * tm * D * 4 + tm * D * 2 + 2 * D * tn * 2 + 2 * tm * tn * 2 + tm * tn * 4
    return pl.pallas_call(
        functools.partial(_in_proj_kernel, tm=tm),
        grid=(N // tm, n_out // tn),
        in_specs=[
            pl.BlockSpec((tm, D), lambda i, j: (i, 0)),
            pl.BlockSpec((1, D), lambda i, j: (0, 0)),
            pl.BlockSpec((None, 1, D), lambda i, j: (i // tpb, 0, 1)),
            pl.BlockSpec((None, 1, D), lambda i, j: (i // tpb, 0, 0)),
            pl.BlockSpec((D, tn), lambda i, j: (0, j)),
        ],
        out_specs=pl.BlockSpec((tm, tn), lambda i, j: (i, j)),
        out_shape=jax.ShapeDtypeStruct((N, n_out), BF16),
        scratch_shapes=[pltpu.VMEM((tm, D), BF16)],
        compiler_params=_params(vmem, 2),
        name="in_proj",
    )(x2d, norm_g.reshape(1, D), mod3, mod3, w_bf16)


def _attn_kernel(slope_ref, lam_ref, q_ref, k_ref, v_ref, g_ref, o_ref,
                 qa_ref, ek_ref, vt_ref, s_ref, m_ref, l_ref, acc_ref, *, tq, d, out_scale):
    h = pl.program_id(1)
    qi = pl.program_id(2)
    lam = lam_ref[0]
    tk = tq
    hw = 2 * d
    nkb = vt_ref.shape[0]
    slope2 = slope_ref[h] * LOG2E

    @pl.when(qi == 0)
    def _():
        for c in range(nkb):
            vt_ref[c] = v_ref[c * tk:(c + 1) * tk, :].astype(F32).T.astype(BF16)
        jj = lax.broadcasted_iota(jnp.int32, (tk, hw), 0)
        ln = lax.broadcasted_iota(jnp.int32, (tk, hw), 1)
        hi = ((jj // 32) * 32).astype(F32)
        lo = (jj % 32).astype(F32)
        ek_ref[...] = jnp.where(ln < 3, hi, jnp.where(ln < 6, lo, 0.0)).astype(BF16)

    qf = q_ref[...].astype(F32) * (d ** -0.5 * LOG2E)
    lane = lax.broadcasted_iota(jnp.int32, (tq, hw), 1)
    qa_ref[0:tq, 0:hw] = jnp.where(lane < d, qf, 0.0).astype(BF16)
    qa_ref[tq:2 * tq, 0:hw] = jnp.where(lane >= d, qf, 0.0).astype(BF16)
    sl = jnp.full((1, hw), slope2, F32)
    pa = sl.astype(BF16).astype(F32)
    ra = sl - pa
    pb = ra.astype(BF16).astype(F32)
    pc = (ra - pb).astype(BF16).astype(F32)
    l1 = lax.broadcasted_iota(jnp.int32, (1, hw), 1)
    piece = jnp.where(l1 % 3 == 0, pa, jnp.where(l1 % 3 == 1, pb, pc))
    eq = jnp.where(l1 < 6, piece, 0.0)
    qa_ref[:, hw:2 * hw] = jnp.broadcast_to(eq, (2 * tq, hw)).astype(BF16)

    m_ref[...] = jnp.full(m_ref.shape, NEG_BIG, F32)
    l_ref[...] = jnp.zeros(l_ref.shape, F32)
    acc_ref[...] = jnp.zeros(acc_ref.shape, F32)

    def scores(kb, slot):
        k = k_ref[pl.ds(pl.multiple_of(kb * tk, tk), tk), :]
        ka = jnp.concatenate([k, ek_ref[...]], axis=1)
        s_ref[slot] = lax.dot_general(ka, qa_ref[...], (((1,), (1,)), ((), ())),
                                      preferred_element_type=F32)

    def softmax_pv(kb, slot, masked):
        s = s_ref[slot]
        if masked:
            key = lax.broadcasted_iota(jnp.int32, (tk, 2 * tq), 0)
            qry = lax.broadcasted_iota(jnp.int32, (tk, 2 * tq), 1)
            qry = jnp.where(qry >= tq, qry - tq, qry)
            s = jnp.where(key <= qry, s, NEG_BIG)
        off = slope2 * ((kb - qi) * tk).astype(F32)
        m_old = m_ref[...]
        m_new = jnp.maximum(m_old, jnp.max(s, axis=0, keepdims=True) + off)
        alpha = jnp.exp2(m_old - m_new)
        p = jnp.exp2(s - (m_new - off))
        l_ref[...] = alpha * l_ref[...] + jnp.sum(p, axis=0, keepdims=True)
        pv = jnp.dot(vt_ref[kb], p.astype(BF16), preferred_element_type=F32)
        acc_ref[...] = alpha * acc_ref[...] + pv
        m_ref[...] = m_new

    scores(0, 0)

    def pair(i, carry):
        kb = 2 * i
        scores(kb + 1, 1)
        softmax_pv(kb, 0, False)
        scores(kb + 2, 0)
        softmax_pv(kb + 1, 1, False)
        return carry

    lax.fori_loop(0, qi // 2, pair, 0)

    @pl.when(qi % 2 == 0)
    def _():
        softmax_pv(qi, 0, True)

    @pl.when(qi % 2 == 1)
    def _():
        scores(qi, 1)
        softmax_pv(qi - 1, 0, False)
        softmax_pv(qi, 1, True)

    inv_l = 1.0 / l_ref[...]
    o = acc_ref[:, 0:tq] * inv_l[:, 0:tq] - lam * (acc_ref[:, tq:2 * tq] * inv_l[:, tq:2 * tq])
    ms = jnp.mean(o * o, axis=0, keepdims=True)
    y = (o * lax.rsqrt(ms + EPS)).T
    o_ref[...] = (y * g_ref[...] * out_scale).astype(o_ref.dtype)


def _diff_attention(proj, lam, subln_g, batch, seq_len, n_heads, d, lam_init):
    N = proj.shape[0]
    hw = 2 * d
    assert hw == V7X_LANES
    tq = _tile(seq_len, ATTN_BLOCK)
    assert tq % 32 == 0 and tq <= 32 * 256
    nq = seq_len // tq
    slopes = jnp.exp2(-8.0 * jnp.arange(1, n_heads + 1, dtype=F32) / n_heads)
    vmem = (2 * 2 * tq * hw * 2 + 2 * 2 * seq_len * hw * 2 + seq_len * hw * 2
            + 2 * tq * 2 * hw * 2 + tq * hw * 2 + 2 * tq * 2 * tq * 4
            + hw * 2 * tq * 4 + 3 * tq * 2 * tq * 4)
    grid_spec = pltpu.PrefetchScalarGridSpec(
        num_scalar_prefetch=2,
        grid=(batch, n_heads, nq),
        in_specs=[
            pl.BlockSpec((tq, hw), lambda b, h, qi, *_: (b * nq + qi, h)),
            pl.BlockSpec((seq_len, hw), lambda b, h, qi, *_: (b, n_heads + h)),
            pl.BlockSpec((seq_len, hw), lambda b, h, qi, *_: (b, 2 * n_heads + h)),
            pl.BlockSpec((1, hw), lambda b, h, qi, *_: (0, 0)),
        ],
        out_specs=pl.BlockSpec((tq, hw), lambda b, h, qi, *_: (b * nq + qi, h)),
        scratch_shapes=[
            pltpu.VMEM((2 * tq, 2 * hw), BF16),
            pltpu.VMEM((tq, hw), BF16),
            pltpu.VMEM((seq_len // tq, hw, tq), BF16),
            pltpu.VMEM((2, tq, 2 * tq), F32),
            pltpu.VMEM((1, 2 * tq), F32),
            pltpu.VMEM((1, 2 * tq), F32),
            pltpu.VMEM((hw, 2 * tq), F32),
        ],
    )
    return pl.pallas_call(
        functools.partial(_attn_kernel, tq=tq, d=d, out_scale=1.0 - lam_init),
        grid_spec=grid_spec,
        out_shape=jax.ShapeDtypeStruct((N, n_heads * hw), BF16),
        compiler_params=_params(vmem, 3),
        name="diff_attn",
    )(slopes, lam.reshape(1), proj, proj, proj, subln_g.reshape(1, hw))


def _s5_tables(a_re, a_im, b_re, b_im, c_re, c_im, d_skip, log_dt):
    hp = lax.Precision.HIGHEST
    G, P = a_re.shape
    cch = b_re.shape[-1]
    T = S5_CHUNK
    lam = lax.complex(a_re.astype(F32), a_im.astype(F32))
    dt = jnp.exp(log_dt.astype(F32))[:, None]
    lam_dt = lam * dt
    a_bar = jnp.exp(lam_dt)
    b_bar = ((a_bar - 1.0) / lam)[..., None] * lax.complex(b_re.astype(F32), b_im.astype(F32))
    c_mat = lax.complex(c_re.astype(F32), c_im.astype(F32))
    pw = jnp.exp(lam_dt[None] * jnp.arange(0, T + 1, dtype=F32)[:, None, None])

    w_c = jnp.einsum('sgp,gpc->gscp', pw[T - 1 - jnp.arange(T)], b_bar).reshape(G, T * cch, P)
    w_tab = jnp.concatenate([w_c.real, w_c.imag, w_c.imag, w_c.real], axis=-1)

    m_c = jnp.einsum('gcp,tgp->gptc', c_mat, pw[1:T + 1]).reshape(G, P, T * cch)
    cp_tab = jnp.concatenate([m_c.real, -m_c.imag], axis=1)

    cr, ci = c_mat.real, c_mat.imag
    ab = pw[:T][:, :, :, None] * b_bar[None]
    k_tab = (jnp.einsum('gcp,kgpd->kgcd', cr, ab.real, precision=hp)
             - jnp.einsum('gcp,kgpd->kgcd', ci, ab.imag, precision=hp))
    k_tab = k_tab.at[0].add(jnp.eye(cch, dtype=F32)[None] * d_skip.astype(F32).reshape(G, cch, 1))
    lag = jnp.arange(T)[None, :] - jnp.arange(T)[:, None]
    tz = k_tab[jnp.clip(lag, 0, T - 1)]
    tz = jnp.where((lag >= 0)[:, :, None, None, None], tz, 0.0)
    tz_tab = tz.transpose(2, 0, 4, 1, 3).reshape(G, T * cch, T * cch)

    at = pw[T]
    a_tab = jnp.stack([
        jnp.concatenate([at.real, at.real], axis=-1),
        jnp.concatenate([-at.imag, at.imag], axis=-1),
        jnp.concatenate([at.imag, -at.imag], axis=-1),
    ], axis=1)
    return w_tab.astype(BF16), cp_tab.astype(BF16), tz_tab.astype(BF16), a_tab.astype(F32)


def _s5_kernel(u_ref, w_ref, cp_ref, tz_ref, a_ref, y_ref, xf_ref, ug_ref, c_ref, s_ref,
               *, gb, cch, n_batch, nch, rb):
    T = S5_CHUNK
    p2 = s_ref.shape[-1]
    lanes = gb * cch
    halves = T // gb
    rows = n_batch * nch
    gran = lax.broadcasted_iota(jnp.int32, (1, lanes), 1) // cch

    def gather_rows(t, carry):
        tok0 = pl.multiple_of(t * (rb * T), rb * T)
        r0 = pl.multiple_of(t * rb, rb)
        xf_ref[...] = u_ref[pl.ds(tok0, rb * T), :].astype(F32)
        x = [xf_ref[pl.ds(s, rb, stride=T), :] for s in range(T)]
        out = [[None] * halves for _ in range(gb)]
        for hf in range(halves):
            for k in range(gb):
                z = jnp.zeros((rb, lanes), F32)
                for s8 in range(gb):
                    z = jnp.where(gran == (s8 - k) % gb, x[hf * gb + s8], z)
                if k:
                    z = pltpu.roll(z, k * cch, axis=1)
                for g in range(gb):
                    prev = out[g][hf] if out[g][hf] is not None else jnp.zeros((rb, lanes), F32)
                    out[g][hf] = jnp.where(gran == (g + k) % gb, z, prev)
        for g in range(gb):
            ug_ref[g, pl.ds(r0, rb), :] = jnp.concatenate(out[g], axis=1).astype(BF16)
        return carry

    lax.fori_loop(0, rows // rb, gather_rows, 0)

    for g in range(gb):
        c_ref[g] = jnp.dot(ug_ref[g], w_ref[g], preferred_element_type=F32)

    a1 = [a_ref[g, 0:1, :] for g in range(gb)]
    a2 = [a_ref[g, 1:2, :] for g in range(gb)]
    a3 = [a_ref[g, 2:3, :] for g in range(gb)]
    chains = [(g, b) for g in range(gb) for b in range(n_batch)]

    def step(j, carry):
        out = []
        for (g, b), (x, xs) in zip(chains, carry):
            r = b * nch + j
            s_ref[g, pl.ds(r, 1), :] = x
            cc = c_ref[g, pl.ds(r, 1), :]
            out.append((a1[g] * x + a2[g] * xs + cc[:, :p2],
                        a1[g] * xs + a3[g] * x + cc[:, p2:]))
        return tuple(out)

    zero = jnp.zeros((1, p2), F32)
    lax.fori_loop(0, nch, step, tuple((zero, zero) for _ in chains))

    for g in range(gb):
        y = jnp.dot(s_ref[g].astype(BF16), cp_ref[g], preferred_element_type=F32)
        c_ref[g] = y + jnp.dot(ug_ref[g], tz_ref[g], preferred_element_type=F32)

    def scatter_rows(t, carry):
        tok0 = pl.multiple_of(t * (rb * T), rb * T)
        r0 = pl.multiple_of(t * rb, rb)
        yg = [[c_ref[g, pl.ds(r0, rb), hf * lanes:(hf + 1) * lanes] for hf in range(halves)]
              for g in range(gb)]
        y = [None] * T
        for hf in range(halves):
            for k in range(gb):
                z = jnp.zeros((rb, lanes), F32)
                for g in range(gb):
                    z = jnp.where(gran == (g + k) % gb, yg[g][hf], z)
                if k:
                    z = pltpu.roll(z, (gb - k) * cch, axis=1)
                for s8 in range(gb):
                    s = hf * gb + s8
                    prev = y[s] if y[s] is not None else jnp.zeros((rb, lanes), F32)
                    y[s] = jnp.where(gran == (s8 - k) % gb, z, prev)
        for s in range(T):
            xf_ref[pl.ds(s, rb, stride=T), :] = y[s]
        y_ref[pl.ds(tok0, rb * T), :] = xf_ref[...].astype(y_ref.dtype)
        return carry

    lax.fori_loop(0, rows // rb, scatter_rows, 0)


def _s5_scan(proj, u_col0, tables, batch, seq_len):
    w_tab, cp_tab, tz_tab, a_tab = tables
    N = proj.shape[0]
    G = w_tab.shape[0]
    tc = w_tab.shape[1]
    p2 = cp_tab.shape[1]
    cch = tc // S5_CHUNK
    nch = seq_len // S5_CHUNK
    rows = batch * nch
    gb = V7X_LANES // cch
    assert G % gb == 0 and S5_CHUNK % gb == 0 and u_col0 % V7X_LANES == 0
    assert 2 * p2 == tc
    rb = _tile(rows, 128)
    u_blk0 = u_col0 // V7X_LANES
    vmem = (2 * 2 * N * V7X_LANES * 2 + 2 * gb * (tc * 2 * p2 + p2 * tc + tc * tc) * 2
            + rb * S5_CHUNK * V7X_LANES * 4 + gb * rows * tc * 2
            + gb * rows * 2 * p2 * 4 + gb * rows * p2 * 4 + 2 * rows * tc * 4)
    return pl.pallas_call(
        functools.partial(_s5_kernel, gb=gb, cch=cch, n_batch=batch, nch=nch, rb=rb),
        grid=(G // gb,),
        in_specs=[
            pl.BlockSpec((N, V7X_LANES), lambda i: (0, u_blk0 + i)),
            pl.BlockSpec((gb, tc, 2 * p2), lambda i: (i, 0, 0)),
            pl.BlockSpec((gb, p2, tc), lambda i: (i, 0, 0)),
            pl.BlockSpec((gb, tc, tc), lambda i: (i, 0, 0)),
            pl.BlockSpec((gb, 3, p2), lambda i: (i, 0, 0)),
        ],
        out_specs=pl.BlockSpec((N, V7X_LANES), lambda i: (0, i)),
        out_shape=jax.ShapeDtypeStruct((N, G * cch), BF16),
        scratch_shapes=[
            pltpu.VMEM((rb * S5_CHUNK, V7X_LANES), F32),
            pltpu.VMEM((gb, rows, tc), BF16),
            pltpu.VMEM((gb, rows, 2 * p2), F32),
            pltpu.VMEM((gb, rows, p2), F32),
        ],
        compiler_params=_params(vmem, 1),
        name="s5_scan",
    )(proj, w_tab, cp_tab, tz_tab, a_tab)


def _gelu_tanh(x):
    return 0.5 * x * (1.0 + jnp.tanh(math.sqrt(2.0 / math.pi) * (x + 0.044715 * (x * x * x))))


def _glu_kernel(y_ref, w_ref, o_ref):
    z = _gelu_tanh(y_ref[...].astype(F32))
    gate = jnp.dot(z.astype(BF16), w_ref[...], preferred_element_type=F32)
    o_ref[...] = (z * jax.nn.sigmoid(gate)).astype(o_ref.dtype)


def _glu(y, w_bf16):
    N, W = y.shape
    tm = _tile(N, 512)
    vmem = 2 * tm * W * 2 + 2 * W * W * 2 + 2 * tm * W * 2 + 4 * tm * W * 4
    return pl.pallas_call(
        _glu_kernel,
        grid=(N // tm,),
        in_specs=[pl.BlockSpec((tm, W), lambda i: (i, 0)),
                  pl.BlockSpec((W, W), lambda i: (0, 0))],
        out_specs=pl.BlockSpec((tm, W), lambda i: (i, 0)),
        out_shape=jax.ShapeDtypeStruct((N, W), BF16),
        compiler_params=_params(vmem, 1),
        name="glu",
    )(y, w_bf16)


def _merge_kernel(a_ref, y_ref, wg_ref, wa_ref, ws_ref, ga_ref, gs_ref, o_ref, s_ref, *, tm):
    @pl.when(pl.program_id(1) == 0)
    def _():
        rows = min(256, tm)

        def body(r, carry):
            sl = pl.ds(pl.multiple_of(r * rows, rows), rows)
            z = _gelu_tanh(y_ref[sl, :].astype(F32))
            gate = jnp.dot(z.astype(BF16), wg_ref[...], preferred_element_type=F32)
            s_ref[sl, :] = (z * jax.nn.sigmoid(gate)).astype(BF16)
            return carry

        lax.fori_loop(0, tm // rows, body, 0)

    ya = jnp.dot(a_ref[...], wa_ref[...], preferred_element_type=F32)
    ys = jnp.dot(s_ref[...], ws_ref[...], preferred_element_type=F32)
    o = jax.nn.sigmoid(ga_ref[...].astype(F32)) * ya + jax.nn.sigmoid(gs_ref[...].astype(F32)) * ys
    o_ref[...] = o.astype(o_ref.dtype)


def _merge(attn, y, wg_bf16, wa_bf16, ws_bf16, proj, ga_col0, gs_col0):
    N, AW = attn.shape
    SW = y.shape[1]
    D = wa_bf16.shape[1]
    tm = _tile(N, 1024)
    tn = _tile(D, 512)
    assert ga_col0 % tn == 0 and gs_col0 % tn == 0
    ga_blk, gs_blk = ga_col0 // tn, gs_col0 // tn
    vmem = (2 * tm * (AW + SW) * 2 + SW * SW * 2 + tm * SW * 2 + 2 * (AW + SW) * tn * 2
            + 3 * 2 * tm * tn * 2 + 3 * tm * tn * 4)
    return pl.pallas_call(
        functools.partial(_merge_kernel, tm=tm),
        grid=(N // tm, D // tn),
        scratch_shapes=[pltpu.VMEM((tm, SW), BF16)],
        in_specs=[
            pl.BlockSpec((tm, AW), lambda i, j: (i, 0)),
            pl.BlockSpec((tm, SW), lambda i, j: (i, 0)),
            pl.BlockSpec((SW, SW), lambda i, j: (0, 0), pipeline_mode=pl.Buffered(1)),
            pl.BlockSpec((AW, tn), lambda i, j: (0, j)),
            pl.BlockSpec((SW, tn), lambda i, j: (0, j)),
            pl.BlockSpec((tm, tn), lambda i, j: (i, ga_blk + j)),
            pl.BlockSpec((tm, tn), lambda i, j: (i, gs_blk + j)),
        ],
        out_specs=pl.BlockSpec((tm, tn), lambda i, j: (i, j)),
        out_shape=jax.ShapeDtypeStruct((N, D), BF16),
        compiler_params=_params(vmem, 2),
        name="merge",
    )(attn, y, wg_bf16, wa_bf16, ws_bf16, proj, proj)


def _out_proj_kernel(m_ref, w_ref, x_ref, g_ref, o_ref):
    o_ref[...] = x_ref[...] + g_ref[...] * jnp.dot(m_ref[...], w_ref[...], preferred_element_type=F32)


def _out_proj(merged, w_bf16, x2d, mod3, seq_len):
    N, D = x2d.shape
    K = merged.shape[1]
    tm = _tile(seq_len, 1024)
    tn = _tile(D, 512)
    tpb = seq_len // tm
    g1_blk = 2 * (D // tn)
    vmem = 2 * tm * K * 2 + 2 * K * tn * 2 + 4 * tm * tn * 4 + tm * tn * 4
    return pl.pallas_call(
        _out_proj_kernel,
        grid=(N // tm, D // tn),
        in_specs=[
            pl.BlockSpec((tm, K), lambda i, j: (i, 0)),
            pl.BlockSpec((K, tn), lambda i, j: (0, j)),
            pl.BlockSpec((tm, tn), lambda i, j: (i, j)),
            pl.BlockSpec((None, 1, tn), lambda i, j: (i // tpb, 0, g1_blk + j)),
        ],
        out_specs=pl.BlockSpec((tm, tn), lambda i, j: (i, j)),
        out_shape=jax.ShapeDtypeStruct((N, D), F32),
        compiler_params=_params(vmem, 2),
        name="out_proj",
    )(merged, w_bf16, x2d, mod3)


def _ffn_up_kernel(x_ref, halo_ref, g_ref, sc_ref, sh_ref, wa_ref, wg_ref, cwa_ref, cwg_ref,
                   o_ref, h_ref, ua_ref, ug_ref, *, tm, halo, seq_len, taps):
    i = pl.program_id(0)

    @pl.when(pl.program_id(1) == 0)
    def _():
        _adaln_rows(x_ref, g_ref, sc_ref, sh_ref, h_ref, tm, halo)
        _adaln_rows(halo_ref, g_ref, sc_ref, sh_ref, h_ref, halo, 0)

        @pl.when((i * tm) % seq_len == 0)
        def _():
            h_ref[0:halo, :] = jnp.zeros((halo, h_ref.shape[1]), BF16)

    h = h_ref[...]
    ua_ref[...] = jnp.dot(h, wa_ref[...], preferred_element_type=F32)
    ug_ref[...] = jnp.dot(h, wg_ref[...], preferred_element_type=F32)

    def conv(u_ref, cw_ref):
        out = cw_ref[taps:taps + 1, :]
        for j in range(taps):
            out = out + cw_ref[j:j + 1, :] * u_ref[pl.ds(halo - (taps - 1) + j, tm), :]
        return out

    a = conv(ua_ref, cwa_ref)
    g = conv(ug_ref, cwg_ref)
    o_ref[...] = (g * jax.nn.sigmoid(g) * a).astype(o_ref.dtype)


def _ffn_up(x1, norm_g, mod3, w_up_bf16, conv_tab, seq_len, ffp, taps):
    N, D = x1.shape
    tm = _tile(seq_len, 1024)
    tn = _tile(ffp, 512)
    halo = BF16_SUBLANES
    assert taps - 1 <= halo and tm % halo == 0
    tpb = seq_len // tm
    nj = ffp // tn
    hb = tm // halo
    vmem = (tm * D * 4 + 2 * halo * D * 4 + (tm + halo) * D * 2 + 2 * 2 * D * tn * 2
            + 2 * tm * tn * 2 + 2 * (tm + halo) * tn * 4 + 4 * tm * tn * 4)
    return pl.pallas_call(
        functools.partial(_ffn_up_kernel, tm=tm, halo=halo, seq_len=seq_len, taps=taps),
        grid=(N // tm, nj),
        in_specs=[
            pl.BlockSpec((tm, D), lambda i, j: (i, 0), pipeline_mode=pl.Buffered(1)),
            pl.BlockSpec((halo, D), lambda i, j: (jnp.maximum(i * hb - 1, 0), 0)),
            pl.BlockSpec((1, D), lambda i, j: (0, 0)),
            pl.BlockSpec((None, 1, D), lambda i, j: (i // tpb, 0, 4)),
            pl.BlockSpec((None, 1, D), lambda i, j: (i // tpb, 0, 3)),
            pl.BlockSpec((D, tn), lambda i, j: (0, j)),
            pl.BlockSpec((D, tn), lambda i, j: (0, nj + j)),
            pl.BlockSpec((8, tn), lambda i, j: (0, j)),
            pl.BlockSpec((8, tn), lambda i, j: (0, nj + j)),
        ],
        out_specs=pl.BlockSpec((tm, tn), lambda i, j: (i, j)),
        out_shape=jax.ShapeDtypeStruct((N, ffp), BF16),
        scratch_shapes=[
            pltpu.VMEM((tm + halo, D), BF16),
            pltpu.VMEM((tm + halo, tn), F32),
            pltpu.VMEM((tm + halo, tn), F32),
        ],
        compiler_params=_params(vmem, 2),
        name="ffn_up",
    )(x1, x1, norm_g.reshape(1, D), mod3, mod3, w_up_bf16, w_up_bf16, conv_tab, conv_tab)


def _ffn_down_kernel(a_ref, w_ref, x_ref, g_ref, fg_ref, o_ref, *, tm, nk, final_norm):
    k = pl.program_id(1)

    @pl.when(k == 0)
    def _():
        o_ref[...] = jnp.zeros(o_ref.shape, F32)

    o_ref[...] += jnp.dot(a_ref[...], w_ref[...], preferred_element_type=F32)

    @pl.when(k == nk - 1)
    def _():
        gate = g_ref[...]
        fg = fg_ref[...]
        rows = min(NORM_ROWS, tm)

        def body(r, carry):
            sl = pl.ds(r * rows, rows)
            x2 = x_ref[sl, :] + gate * o_ref[sl, :]
            if final_norm:
                ms = jnp.mean(x2 * x2, axis=-1, keepdims=True)
                x2 = (x2 * lax.rsqrt(ms + EPS)) * fg
            o_ref[sl, :] = x2
            return carry

        lax.fori_loop(0, tm // rows, body, 0)


def _ffn_down(act, w_bf16, x1, mod3, final_g, seq_len, final_norm):
    N, D = x1.shape
    K = act.shape[1]
    tm = _tile(seq_len, 512)
    tk = _tile(K, 1024)
    tpb = seq_len // tm
    nk = K // tk
    vmem = 2 * tm * tk * 2 + 2 * tk * D * 2 + tm * D * 4 + 2 * tm * D * 4 + tm * D * 4
    return pl.pallas_call(
        functools.partial(_ffn_down_kernel, tm=tm, nk=nk, final_norm=final_norm),
        grid=(N // tm, nk),
        in_specs=[
            pl.BlockSpec((tm, tk), lambda i, k: (i, k)),
            pl.BlockSpec((tk, D), lambda i, k: (k, 0)),
            pl.BlockSpec((tm, D), lambda i, k: (i, 0), pipeline_mode=pl.Buffered(1)),
            pl.BlockSpec((None, 1, D), lambda i, k: (i // tpb, 0, 5)),
            pl.BlockSpec((1, D), lambda i, k: (0, 0)),
        ],
        out_specs=pl.BlockSpec((tm, D), lambda i, k: (i, 0)),
        out_shape=jax.ShapeDtypeStruct((N, D), F32),
        compiler_params=_params(vmem, 2),
        name="ffn_down",
    )(act, w_bf16, x1, mod3, final_g.reshape(1, D))


def _round_up(n, m):
    return (n + m - 1) // m * m


def kernel(x, c, ada_w, ada_b, norm1_g, w_in, lq1, lk1, lq2, lk2, subln_g, a_re, a_im, b_re, b_im,
           c_re, c_im, d_skip, log_dt, w_glu, w_attn_br, w_ssm_br, w_out, norm2_g, w_up, conv_w,
           conv_b, w_down, final_g):
    B, L, D = x.shape
    depth = ada_w.shape[0]
    d = lq1.shape[-1]
    aw = w_attn_br.shape[1]
    sw = w_ssm_br.shape[1]
    n_heads = aw // (2 * d)
    ff = w_down.shape[1]
    taps = conv_w.shape[1]
    ffp = _round_up(ff, 1024) if ff > 1024 else ff
    u_col0 = 3 * aw
    ga_col0 = u_col0 + sw
    gs_col0 = ga_col0 + D

    xc = x.reshape(B * L, D)
    for l in range(depth):
        lam_init = 0.8 - 0.6 * math.exp(-0.3 * l)
        lam = (jnp.exp(jnp.sum(lq1[l].astype(F32) * lk1[l].astype(F32)))
               - jnp.exp(jnp.sum(lq2[l].astype(F32) * lk2[l].astype(F32))) + lam_init)
        mod3 = _ada_mod(c, ada_w[l], ada_b[l])

        proj = _in_proj(xc, norm1_g[l], mod3, w_in[l].astype(BF16), L)
        attn = _diff_attention(proj, lam, subln_g[l], B, L, n_heads, d, lam_init)
        tables = _s5_tables(a_re[l], a_im[l], b_re[l], b_im[l], c_re[l], c_im[l], d_skip[l], log_dt[l])
        y = _s5_scan(proj, u_col0, tables, B, L)
        merged = _merge(attn, y, w_glu[l].astype(BF16), w_attn_br[l].astype(BF16),
                        w_ssm_br[l].astype(BF16), proj, ga_col0, gs_col0)
        x1 = _out_proj(merged, w_out[l].astype(BF16), xc, mod3, L)

        pad = ffp - ff
        w_up_p = jnp.concatenate([jnp.pad(w_up[l][:, :ff], ((0, 0), (0, pad))),
                                  jnp.pad(w_up[l][:, ff:], ((0, 0), (0, pad)))], axis=1).astype(BF16)
        conv_rows = jnp.concatenate([conv_w[l], conv_b[l][None]], axis=0).astype(F32)
        conv_tab = jnp.concatenate([jnp.pad(conv_rows[:, :ff], ((0, 8 - taps - 1), (0, pad))),
                                    jnp.pad(conv_rows[:, ff:], ((0, 8 - taps - 1), (0, pad)))], axis=1)
        w_down_p = jnp.pad(w_down[l], ((0, pad), (0, 0))).astype(BF16)
        act = _ffn_up(x1, norm2_g[l], mod3, w_up_p, conv_tab, L, ffp, taps)
        xc = _ffn_down(act, w_down_p, x1, mod3, final_g, L, final_norm=(l == depth - 1))
    return xc.reshape(B, L, D)
```
